```python
import math
import jax, jax.numpy as jnp
from jax import lax
import numpy as np

D_MODEL = 1024
BATCH = 8
SEQ = 8192
DEPTH = 1
DEC_BATCH = 128
DEC_SEQ = 1
PAST_LEN = 8192
PAGE_SIZE = 128

N_HEADS = 8
HEAD_DIM = 64
D_ATTN = N_HEADS * HEAD_DIM
D_RNN = D_MODEL
N_RNN_BLOCKS = 16
RNN_BLOCK = D_RNN // N_RNN_BLOCKS
CONV_WIDTH = 4
LRU_C = 8.0
D_FF = 4 * D_MODEL
D_PLE = 256
Q_BLOCK = 128
RMS_EPS = 1e-6
IN_SIZES = (D_ATTN, D_ATTN, D_ATTN, N_HEADS, D_RNN, D_RNN, D_MODEL, D_MODEL)
D_IN = 3 * D_ATTN + N_HEADS + 2 * D_RNN + 2 * D_MODEL

kernel_name = "fox_rglru_parallel_hybrid_step"


def _rmsnorm(x, g):
    xf = x.astype(jnp.float32)
    y = xf * lax.rsqrt(jnp.mean(xf * xf, axis=-1, keepdims=True) + RMS_EPS)
    return (y * g.astype(jnp.float32)).astype(x.dtype)


def _split_in(z):
    out, start = [], 0
    for s in IN_SIZES:
        out.append(z[..., start:start + s])
        start += s
    return out


def _attend(q, k, v, cq, ck, qpos, kpos):
    s = jnp.einsum('bqhd,bkhd->bhqk', q, k).astype(jnp.float32) / math.sqrt(HEAD_DIM)
    s = s + (jnp.swapaxes(cq, 1, 2)[..., :, None] - jnp.swapaxes(ck, 1, 2)[..., None, :])
    mask = kpos[None, :] <= qpos[:, None]
    s = jnp.where(mask[None, None], s, -jnp.inf)
    p = jax.nn.softmax(s, axis=-1)
    return jnp.einsum('bhqk,bkhd->bqhd', p.astype(v.dtype), v)


def _prompt_attention(q, k, v, logf):
    B, S = q.shape[0], q.shape[1]
    c = jnp.cumsum(logf, axis=1)
    nb = S // Q_BLOCK
    kpos = jnp.arange(S)
    qb = jnp.swapaxes(q.reshape(B, nb, Q_BLOCK, N_HEADS, HEAD_DIM), 0, 1)
    cb = jnp.swapaxes(c.reshape(B, nb, Q_BLOCK, N_HEADS), 0, 1)
    pb = kpos.reshape(nb, Q_BLOCK)
    ob = lax.map(lambda a: _attend(a[0], k, v, a[1], c, a[2], kpos), (qb, cb, pb))
    return jnp.swapaxes(ob, 0, 1).reshape(B, S, D_ATTN)


def _sample_attention(q, k, v, logf, ck_pool, cv_pool, cl_pool, page_table):
    DB, T = q.shape[0], q.shape[1]
    past = page_table.shape[1] * PAGE_SIZE
    kp = ck_pool[page_table].reshape(DB, past, N_HEADS, HEAD_DIM).astype(k.dtype)
    vp = cv_pool[page_table].reshape(DB, past, N_HEADS, HEAD_DIM).astype(v.dtype)
    lp = cl_pool[page_table].reshape(DB, past, N_HEADS).astype(jnp.float32)
    k_all = jnp.concatenate([kp, k], axis=1)
    v_all = jnp.concatenate([vp, v], axis=1)
    c = jnp.cumsum(jnp.concatenate([lp, logf], axis=1), axis=1)
    kpos = jnp.arange(past + T)
    qpos = past + jnp.arange(T)
    o = _attend(q, k_all, v_all, c[:, past:], c, qpos, kpos)
    return o.reshape(DB, T, D_ATTN)


def _causal_conv(xpad, w, b, T):
    out = b
    for j in range(CONV_WIDTH):
        out = out + xpad[:, j:j + T] * w[j]
    return out


def _rglru(x, h0, w_rg, b_rg, w_ig, b_ig, a_param, reset_first):
    B, T = x.shape[0], x.shape[1]
    xb = x.reshape(B, T, N_RNN_BLOCKS, RNN_BLOCK)
    r = jax.nn.sigmoid(jnp.einsum('btnd,nde->btne', xb, w_rg).reshape(B, T, D_RNN).astype(jnp.float32) + b_rg.astype(jnp.float32))
    i = jax.nn.sigmoid(jnp.einsum('btnd,nde->btne', xb, w_ig).reshape(B, T, D_RNN).astype(jnp.float32) + b_ig.astype(jnp.float32))
    log_a = -LRU_C * r * jax.nn.softplus(a_param.astype(jnp.float32))
    a = jnp.exp(log_a)
    mult = jnp.sqrt(-jnp.expm1(2.0 * log_a))
    if reset_first:
        mult = mult.at[:, 0].set(1.0)
    u = mult * i * x.astype(jnp.float32)

    def step(h, inp):
        a_t, u_t = inp
        h = a_t * h + u_t
        return h, h

    h_last, hs = lax.scan(step, h0, (jnp.swapaxes(a, 0, 1), jnp.swapaxes(u, 0, 1)))
    return jnp.swapaxes(hs, 0, 1).astype(x.dtype), h_last


def _layer(x, pl, attn_fn, conv_prefix, h0, reset_first, lw):
    (g_mix_pre, w_in, b_f, w_rg, b_rg, w_ig, b_ig, a_param, conv_w, conv_b,
     w_attn_out, w_rnn_out, w_out, g_mix_post, g_mlp_pre, w_ff1, w_ff2, g_mlp_post,
     g_ple_pre, w_ple_gate, w_ple, g_ple_post) = lw
    B, T = x.shape[0], x.shape[1]
    xn = _rmsnorm(x, g_mix_pre)
    z = xn @ w_in
    q, k, v, f_logit, xr, gr, ga_logit, grr_logit = _split_in(z)
    q = q.reshape(B, T, N_HEADS, HEAD_DIM)
    k = k.reshape(B, T, N_HEADS, HEAD_DIM)
    v = v.reshape(B, T, N_HEADS, HEAD_DIM)
    logf = jax.nn.log_sigmoid(f_logit.astype(jnp.float32) + b_f.astype(jnp.float32))
    attn = attn_fn(q, k, v, logf)
    xpad = jnp.concatenate([conv_prefix.astype(xr.dtype), xr], axis=1)
    conv_state = xpad[:, -(CONV_WIDTH - 1):]
    xc = _causal_conv(xpad, conv_w, conv_b, T)
    y, h_last = _rglru(xc, h0, w_rg, b_rg, w_ig, b_ig, a_param, reset_first)
    rnn = y * jax.nn.gelu(gr)
    m = jax.nn.sigmoid(ga_logit) * (attn @ w_attn_out) + jax.nn.sigmoid(grr_logit) * (rnn @ w_rnn_out)
    x = x + _rmsnorm(m @ w_out, g_mix_post)
    hn = _rmsnorm(x, g_mlp_pre)
    x = x + _rmsnorm(jnp.square(jax.nn.relu(hn @ w_ff1)) @ w_ff2, g_mlp_post)
    gate = jax.nn.sigmoid(_rmsnorm(x, g_ple_pre) @ w_ple_gate)
    x = x + _rmsnorm((pl.astype(x.dtype) @ w_ple) * gate, g_ple_post)
    return x, (k, v, logf, conv_state, h_last)


def setup_inputs(seed: int = 0) -> dict:
    key = jax.random.key(seed)
    ks = iter(jax.random.split(key, 48))
    f32 = jnp.float32
    n_pages = PAST_LEN // PAGE_SIZE
    n_used = DEC_BATCH * n_pages
    n_pool = n_used + n_used // 4

    def nrm(shape, scale):
        return jax.random.normal(next(ks), shape, f32) * scale

    def gain(shape):
        return 1.0 + 0.02 * jax.random.normal(next(ks), shape, f32)

    x_prompt = nrm((BATCH, SEQ, D_MODEL), 1.0)
    x_sample = nrm((DEC_BATCH, DEC_SEQ, D_MODEL), 1.0)
    p_prompt = nrm((DEPTH, BATCH, SEQ, D_PLE), 1.0)
    p_sample = nrm((DEPTH, DEC_BATCH, DEC_SEQ, D_PLE), 1.0)
    cache_k = nrm((DEPTH, n_pool, PAGE_SIZE, N_HEADS, HEAD_DIM), 1.0)
    cache_v = nrm((DEPTH, n_pool, PAGE_SIZE, N_HEADS, HEAD_DIM), 1.0)
    cache_logf = jax.nn.log_sigmoid(jax.random.uniform(next(ks), (DEPTH, n_pool, PAGE_SIZE, N_HEADS), f32, 2.0, 6.0))
    state_conv = nrm((DEPTH, DEC_BATCH, CONV_WIDTH - 1, D_RNN), 1.0)
    state_h = nrm((DEPTH, DEC_BATCH, D_RNN), 0.5)
    page_table = jax.random.permutation(next(ks), n_pool)[:n_used].astype(jnp.int32).reshape(DEC_BATCH, n_pages)

    g_mix_pre = gain((DEPTH, D_MODEL))
    w_in = nrm((DEPTH, D_MODEL, D_IN), D_MODEL ** -0.5)
    b_f = jax.random.uniform(next(ks), (DEPTH, N_HEADS), f32, 2.0, 6.0)
    w_rg = nrm((DEPTH, N_RNN_BLOCKS, RNN_BLOCK, RNN_BLOCK), RNN_BLOCK ** -0.5)
    b_rg = nrm((DEPTH, D_RNN), 0.01)
    w_ig = nrm((DEPTH, N_RNN_BLOCKS, RNN_BLOCK, RNN_BLOCK), RNN_BLOCK ** -0.5)
    b_ig = nrm((DEPTH, D_RNN), 0.01)
    rad2 = jax.random.uniform(next(ks), (DEPTH, D_RNN), f32, 0.9 ** 2, 0.999 ** 2)
    a_param = jnp.log(jnp.expm1(-0.5 * jnp.log(rad2)))
    conv_w = nrm((DEPTH, CONV_WIDTH, D_RNN), CONV_WIDTH ** -0.5)
    conv_b = nrm((DEPTH, D_RNN), 0.01)
    w_attn_out = nrm((DEPTH, D_ATTN, D_MODEL), D_ATTN ** -0.5)
    w_rnn_out = nrm((DEPTH, D_RNN, D_MODEL), D_RNN ** -0.5)
    w_out = nrm((DEPTH, D_MODEL, D_MODEL), D_MODEL ** -0.5)
    g_mix_post = gain((DEPTH, D_MODEL))
    g_mlp_pre = gain((DEPTH, D_MODEL))
    w_ff1 = nrm((DEPTH, D_MODEL, D_FF), D_MODEL ** -0.5)
    w_ff2 = nrm((DEPTH, D_FF, D_MODEL), D_FF ** -0.5)
    g_mlp_post = gain((DEPTH, D_MODEL))
    g_ple_pre = gain((DEPTH, D_MODEL))
    w_ple_gate = nrm((DEPTH, D_MODEL, D_MODEL), D_MODEL ** -0.5)
    w_ple = nrm((DEPTH, D_PLE, D_MODEL), D_PLE ** -0.5)
    g_ple_post = gain((DEPTH, D_MODEL))
    return {
        "x_prompt": x_prompt, "x_sample": x_sample, "p_prompt": p_prompt, "p_sample": p_sample,
        "cache_k": cache_k, "cache_v": cache_v, "cache_logf": cache_logf,
        "state_conv": state_conv, "state_h": state_h, "page_table": page_table,
        "g_mix_pre": g_mix_pre, "w_in": w_in, "b_f": b_f, "w_rg": w_rg, "b_rg": b_rg,
        "w_ig": w_ig, "b_ig": b_ig, "a_param": a_param, "conv_w": conv_w, "conv_b": conv_b,
        "w_attn_out": w_attn_out, "w_rnn_out": w_rnn_out, "w_out": w_out, "g_mix_post": g_mix_post,
        "g_mlp_pre": g_mlp_pre, "w_ff1": w_ff1, "w_ff2": w_ff2, "g_mlp_post": g_mlp_post,
        "g_ple_pre": g_ple_pre, "w_ple_gate": w_ple_gate, "w_ple": w_ple, "g_ple_post": g_ple_post,
    }


def reference(x_prompt, x_sample, p_prompt, p_sample, cache_k, cache_v, cache_logf, state_conv, state_h,
              page_table, g_mix_pre, w_in, b_f, w_rg, b_rg, w_ig, b_ig, a_param, conv_w, conv_b,
              w_attn_out, w_rnn_out, w_out, g_mix_post, g_mlp_pre, w_ff1, w_ff2, g_mlp_post,
              g_ple_pre, w_ple_gate, w_ple, g_ple_post):
    xp, xs = x_prompt, x_sample
    B, DB = x_prompt.shape[0], x_sample.shape[0]
    pk, pv, plf, pc, ph = [], [], [], [], []
    sk, sv, slf, sc, sh = [], [], [], [], []
    for l in range(DEPTH):
        lw = (g_mix_pre[l], w_in[l], b_f[l], w_rg[l], b_rg[l], w_ig[l], b_ig[l], a_param[l],
              conv_w[l], conv_b[l], w_attn_out[l], w_rnn_out[l], w_out[l], g_mix_post[l],
              g_mlp_pre[l], w_ff1[l], w_ff2[l], g_mlp_post[l], g_ple_pre[l], w_ple_gate[l],
              w_ple[l], g_ple_post[l])
        xp, st_p = _layer(xp, p_prompt[l], _prompt_attention,
                          jnp.zeros((B, CONV_WIDTH - 1, D_RNN), xp.dtype),
                          jnp.zeros((B, D_RNN), jnp.float32), True, lw)
        ck_l, cv_l, cl_l = cache_k[l], cache_v[l], cache_logf[l]
        samp_attn = lambda q, k, v, f, ck_l=ck_l, cv_l=cv_l, cl_l=cl_l: _sample_attention(
            q, k, v, f, ck_l, cv_l, cl_l, page_table)
        xs, st_s = _layer(xs, p_sample[l], samp_attn, state_conv[l],
                          state_h[l].astype(jnp.float32), False, lw)
        pk.append(st_p[0]); pv.append(st_p[1]); plf.append(st_p[2]); pc.append(st_p[3]); ph.append(st_p[4])
        sk.append(st_s[0]); sv.append(st_s[1]); slf.append(st_s[2]); sc.append(st_s[3]); sh.append(st_s[4])
    return (xp, xs,
            jnp.stack(pk), jnp.stack(pv), jnp.stack(plf), jnp.stack(pc), jnp.stack(ph),
            jnp.stack(sk), jnp.stack(sv), jnp.stack(slf), jnp.stack(sc), jnp.stack(sh))
```

```python
import functools
import math

import jax
import jax.numpy as jnp
from jax import lax
from jax.experimental import pallas as pl
from jax.experimental.pallas import tpu as pltpu

F32 = jnp.float32
BF16 = jnp.bfloat16
HIGHEST = lax.Precision.HIGHEST

RMS_EPS = 1e-6
LRU_C = 8.0
CONV_WIDTH = 4
RNN_BLOCK = 64
HEAD_DIM = 64
LANES = 128
SUBLANES = 8
MXU_DIM = 256
VMEM_LIMIT = 56 * 1024 * 1024
NEG_BIG = float(jnp.finfo(jnp.float32).min)


def _rms(x, g):
    ms = jnp.mean(x * x, axis=-1, keepdims=True)
    return x * lax.rsqrt(ms + RMS_EPS) * g


def _softplus(x):
    return jnp.maximum(x, 0.0) + jnp.log1p(jnp.exp(-jnp.abs(x)))


def _sigmoid(x):
    return 1.0 / (1.0 + jnp.exp(-x))


def _gelu_tanh(x):
    c = math.sqrt(2.0 / math.pi)
    return 0.5 * x * (1.0 + jnp.tanh(c * (x + 0.044715 * (x * x * x))))


def _const_spec(shape):
    nd = len(shape)
    return pl.BlockSpec(shape, lambda *_: (0,) * nd, pipeline_mode=pl.Buffered(1))


def _params(sem):
    return pltpu.CompilerParams(dimension_semantics=sem, vmem_limit_bytes=VMEM_LIMIT)


def _inproj_kernel(x_ref, g_ref, wqkv_ref, wf_ref, bf_ref,
                   q_ref, k_ref, v_ref, kb_ref, vb_ref, lf_ref, lft_ref, *, d_attn):
    xn = _rms(x_ref[...], g_ref[...]).astype(BF16)
    qkv = jnp.dot(xn, wqkv_ref[...], preferred_element_type=F32)
    q = qkv[:, :d_attn]
    k = qkv[:, d_attn:2 * d_attn]
    v = qkv[:, 2 * d_attn:]
    q_ref[...] = (q * (1.0 / math.sqrt(HEAD_DIM))).astype(BF16)
    k_ref[...] = k
    v_ref[...] = v
    kb_ref[...] = k.astype(BF16)
    vb_ref[...] = v.astype(BF16)
    f = jnp.dot(xn, wf_ref[...], preferred_element_type=F32) + bf_ref[...]
    lf = -_softplus(-f)
    lf_ref[...] = lf
    lft_ref[...] = lf.T[:SUBLANES, :]


def _inproj(x, g, wqkv, wf, bfp, tm):
    n, d = x.shape
    d_attn = wqkv.shape[1] // 3
    tok = lambda w: pl.BlockSpec((tm, w), lambda i: (i, 0))
    return pl.pallas_call(
        functools.partial(_inproj_kernel, d_attn=d_attn),
        grid=(n // tm,),
        in_specs=[tok(d), _const_spec((1, d)), _const_spec(wqkv.shape), _const_spec(wf.shape),
                  _const_spec((1, LANES))],
        out_specs=[tok(d_attn), tok(d_attn), tok(d_attn), tok(d_attn), tok(d_attn), tok(LANES),
                   pl.BlockSpec((SUBLANES, tm), lambda i: (0, i))],
        out_shape=[jax.ShapeDtypeStruct((n, d_attn), BF16),
                   jax.ShapeDtypeStruct((n, d_attn), F32),
                   jax.ShapeDtypeStruct((n, d_attn), F32),
                   jax.ShapeDtypeStruct((n, d_attn), BF16),
                   jax.ShapeDtypeStruct((n, d_attn), BF16),
                   jax.ShapeDtypeStruct((n, LANES), F32),
                   jax.ShapeDtypeStruct((SUBLANES, n), F32)],
        compiler_params=_params(("arbitrary",)),
        name="inproj",
    )(x, g, wqkv, wf, bfp)


def _cumsum_kernel(lf_ref, lft_ref, cc_ref, cr_ref, carry_c, carry_r, *, tc):
    @pl.when(pl.program_id(1) == 0)
    def _():
        carry_c[...] = jnp.zeros_like(carry_c)
        carry_r[...] = jnp.zeros_like(carry_r)

    row = lax.broadcasted_iota(jnp.int32, (tc, tc), 0)
    col = lax.broadcasted_iota(jnp.int32, (tc, tc), 1)
    lower = (col <= row).astype(F32)
    cc = jnp.dot(lower, lf_ref[...], precision=HIGHEST, preferred_element_type=F32) + carry_c[...]
    cc_ref[...] = cc
    carry_c[...] = cc[tc - 1:tc, :]
    upper = (row <= col).astype(F32)
    cr = jnp.dot(lft_ref[...], upper, precision=HIGHEST, preferred_element_type=F32) + carry_r[...]
    cr_ref[...] = cr
    carry_r[...] = cr[:, tc - 1:tc]


def _cumsum(lf, lft, b, s, tc):
    n = b * s
    nt = s // tc
    return pl.pallas_call(
        functools.partial(_cumsum_kernel, tc=tc),
        grid=(b, nt),
        in_specs=[pl.BlockSpec((tc, LANES), lambda i, j: (i * nt + j, 0)),
                  pl.BlockSpec((SUBLANES, tc), lambda i, j: (0, i * nt + j))],
        out_specs=[pl.BlockSpec((tc, LANES), lambda i, j: (i * nt + j, 0)),
                   pl.BlockSpec((SUBLANES, tc), lambda i, j: (0, i * nt + j))],
        out_shape=[jax.ShapeDtypeStruct((n, LANES), F32), jax.ShapeDtypeStruct((SUBLANES, n), F32)],
        scratch_shapes=[pltpu.VMEM((1, LANES), F32), pltpu.VMEM((SUBLANES, 1), F32)],
        compiler_params=_params(("arbitrary", "arbitrary")),
        name="logf_cumsum",
    )(lf, lft)


def _attn_kernel(q_ref, k_ref, v_ref, cc_ref, cr_ref, o_ref, qm_sc, m_sc, l_sc, acc_sc, *, tq):
    hp = pl.program_id(1)
    qi = pl.program_id(2)
    lane = lax.broadcasted_iota(jnp.int32, (tq, LANES), 1)
    q2 = q_ref[...]
    cc = cc_ref[...]
    cqs = []
    for e in range(2):
        in_head = (lane >= HEAD_DIM * e) & (lane < HEAD_DIM * (e + 1))
        qm_sc[e] = jnp.where(in_head, q2, jnp.zeros_like(q2))
        cqs.append(jnp.sum(jnp.where(lane == 2 * hp + e, cc, 0.0), axis=1, keepdims=True))
        m_sc[e] = jnp.full((tq, 1), NEG_BIG, F32)
        l_sc[e] = jnp.zeros((tq, 1), F32)
        acc_sc[e] = jnp.zeros((tq, LANES), F32)

    def step(j, masked):
        k0 = pl.multiple_of(j * tq, tq)
        kb = k_ref[pl.ds(k0, tq), :]
        vb = v_ref[pl.ds(k0, tq), :]
        for e in range(2):
            s = lax.dot_general(qm_sc[e], kb, (((1,), (1,)), ((), ())), preferred_element_type=F32)
            ck = cr_ref[pl.ds(2 * hp + e, 1), pl.ds(k0, tq)]
            s = s + (cqs[e] - ck)
            if masked:
                r = lax.broadcasted_iota(jnp.int32, (tq, tq), 0)
                c = lax.broadcasted_iota(jnp.int32, (tq, tq), 1)
                s = jnp.where(c <= r, s, -jnp.inf)
            m_prev = m_sc[e]
            m_new = jnp.maximum(m_prev, jnp.max(s, axis=1, keepdims=True))
            alpha = jnp.exp(m_prev - m_new)
            p = jnp.exp(s - m_new)
            l_sc[e] = alpha * l_sc[e] + jnp.sum(p, axis=1, keepdims=True)
            acc_sc[e] = alpha * acc_sc[e] + jnp.dot(p.astype(BF16), vb, preferred_element_type=F32)
            m_sc[e] = m_new

    def body(j, carry):
        step(j, False)
        return carry

    lax.fori_loop(0, qi, body, 0)
    step(qi, True)
    o0 = acc_sc[0] / l_sc[0]
    o1 = acc_sc[1] / l_sc[1]
    o_ref[...] = jnp.where(lane < HEAD_DIM, o0, o1).astype(BF16)


def _attention(q, kb, vb, cc, cr, b, s, tq):
    n, d_attn = q.shape
    nq = s // tq
    npairs = d_attn // LANES
    return pl.pallas_call(
        functools.partial(_attn_kernel, tq=tq),
        grid=(b, npairs, nq),
        in_specs=[pl.BlockSpec((tq, LANES), lambda i, h, j: (i * nq + j, h)),
                  pl.BlockSpec((s, LANES), lambda i, h, j: (i, h)),
                  pl.BlockSpec((s, LANES), lambda i, h, j: (i, h)),
                  pl.BlockSpec((tq, LANES), lambda i, h, j: (i * nq + j, 0)),
                  pl.BlockSpec((SUBLANES, s), lambda i, h, j: (0, i))],
        out_specs=pl.BlockSpec((tq, LANES), lambda i, h, j: (i * nq + j, h)),
        out_shape=jax.ShapeDtypeStruct((n, d_attn), BF16),
        scratch_shapes=[pltpu.VMEM((2, tq, LANES), BF16), pltpu.VMEM((2, tq, 1), F32),
                        pltpu.VMEM((2, tq, 1), F32), pltpu.VMEM((2, tq, LANES), F32)],
        compiler_params=_params(("arbitrary", "arbitrary", "arbitrary")),
        name="fox_attention",
    )(q, kb, vb, cc, cr)


def _sample_attn_kernel(pt_ref, q_ref, kn_ref, vn_ref, lfn_ref, *rest, pages_per_step, d_attn):
    del pt_ref
    pp = pages_per_step
    k_refs = rest[:pp]
    v_refs = rest[pp:2 * pp]
    lp_refs = rest[2 * pp:3 * pp]
    o_ref, m_sc, l_sc, acc_sc, carry_sc = rest[3 * pp:]
    g = pl.program_id(1)
    n_heads = d_attn // HEAD_DIM
    head_of_col = lax.broadcasted_iota(jnp.int32, (n_heads, d_attn), 1) // HEAD_DIM
    head_of_row = lax.broadcasted_iota(jnp.int32, (n_heads, d_attn), 0)
    own = head_of_col == head_of_row
    qbd_f = jnp.where(own, jnp.broadcast_to(q_ref[0].astype(F32), (n_heads, d_attn)), 0.0)
    qbd = qbd_f.astype(BF16)

    @pl.when(g == 0)
    def _():
        kn = kn_ref[0].astype(BF16).astype(F32)
        m_sc[...] = jnp.sum(qbd_f * kn, axis=1, keepdims=True)
        l_sc[...] = jnp.ones_like(l_sc)
        acc_sc[...] = jnp.broadcast_to(vn_ref[0].astype(BF16).astype(F32), (n_heads, d_attn))
        carry_sc[...] = lfn_ref[0]

    page = k_refs[0].shape[1]
    later = (lax.broadcasted_iota(jnp.int32, (page, page), 0)
             > lax.broadcasted_iota(jnp.int32, (page, page), 1)).astype(F32)
    for i in range(pp):
        kp = k_refs[i][0].astype(BF16)
        vp = v_refs[i][0].astype(BF16)
        lp = lp_refs[i][0]
        s = lax.dot_general(qbd, kp, (((1,), (1,)), ((), ())), preferred_element_type=F32)
        bias = jnp.dot(lp, later, precision=HIGHEST, preferred_element_type=F32) + carry_sc[...]
        s = s + bias
        m_prev = m_sc[...]
        m_new = jnp.maximum(m_prev, jnp.max(s, axis=1, keepdims=True))
        alpha = jnp.exp(m_prev - m_new)
        p = jnp.exp(s - m_new)
        l_sc[...] = alpha * l_sc[...] + jnp.sum(p, axis=1, keepdims=True)
        acc_sc[...] = alpha * acc_sc[...] + jnp.dot(p.astype(BF16), vp, preferred_element_type=F32)
        m_sc[...] = m_new
        carry_sc[...] = carry_sc[...] + jnp.sum(lp, axis=1, keepdims=True)

    @pl.when(g == pl.num_programs(1) - 1)
    def _():
        o = jnp.where(own, acc_sc[...] / l_sc[...], 0.0)
        o_ref[0] = jnp.sum(o, axis=0, keepdims=True).astype(BF16)


def _sample_attention(q, kn, vn, lfn, ck, cv, clt, page_table, pages_per_step):
    db, d_attn = q.shape
    n_pages = page_table.shape[1]
    page = ck.shape[1]
    n_heads = d_attn // HEAD_DIM
    pp = pages_per_step
    steps = n_pages // pp
    pt = page_table.reshape(-1)

    def page_map(i):
        return lambda b, g, pt_ref: (pt_ref[b * n_pages + (n_pages - 1 - (g * pp + i))], 0, 0)

    tok3 = lambda w: pl.BlockSpec((1, 1, w), lambda b, g, pt_ref: (b, 0, 0))
    in_specs = [tok3(d_attn), tok3(d_attn), tok3(d_attn),
                pl.BlockSpec((1, n_heads, 1), lambda b, g, pt_ref: (b, 0, 0))]
    in_specs += [pl.BlockSpec((1, page, d_attn), page_map(i)) for i in range(pp)]
    in_specs += [pl.BlockSpec((1, page, d_attn), page_map(i)) for i in range(pp)]
    in_specs += [pl.BlockSpec((1, n_heads, page), page_map(i)) for i in range(pp)]
    out = pl.pallas_call(
        functools.partial(_sample_attn_kernel, pages_per_step=pp, d_attn=d_attn),
        grid_spec=pltpu.PrefetchScalarGridSpec(
            num_scalar_prefetch=1,
            grid=(db, steps),
            in_specs=in_specs,
            out_specs=pl.BlockSpec((1, 1, d_attn), lambda b, g, pt_ref: (b, 0, 0)),
            scratch_shapes=[pltpu.VMEM((n_heads, 1), F32), pltpu.VMEM((n_heads, 1), F32),
                            pltpu.VMEM((n_heads, d_attn), F32), pltpu.VMEM((n_heads, 1), F32)]),
        out_shape=jax.ShapeDtypeStruct((db, 1, d_attn), BF16),
        compiler_params=_params(("arbitrary", "arbitrary")),
        name="sample_attention",
    )(pt, q.reshape(db, 1, d_attn), kn.reshape(db, 1, d_attn), vn.reshape(db, 1, d_attn), lfn,
      *([ck] * pp), *([cv] * pp), *([clt] * pp))
    return out.reshape(db, d_attn)


def _lru_coeffs(pre_r, pre_i, xc, brg, big, neg_c_sp):
    r = _sigmoid(pre_r + brg)
    i = _sigmoid(pre_i + big)
    a = jnp.exp(r * neg_c_sp)
    mult = jnp.sqrt(1.0 - a * a)
    return a, mult, i * xc


def _gate_matmul(xcb, wg_ref, pre_ref):
    for j in range(wg_ref.shape[0]):
        lo = j * MXU_DIM
        pre_ref[:, 2 * lo:2 * lo + 2 * MXU_DIM] = jnp.dot(
            xcb[:, lo:lo + MXU_DIM], wg_ref[j], preferred_element_type=F32)


def _rnn_prompt_kernel(x_ref, g_ref, wxr_ref, wgr_ref, wg_ref, brg_ref, big_ref, ap_ref, cw_ref,
                       cb_ref, rnn_ref, conv_ref, h_ref,
                       xr_sc, xc_sc, pre_sc, gg_sc, hcar_sc, *, tt):
    t = pl.program_id(1)
    d = x_ref.shape[1]
    ngroups = tt // SUBLANES

    @pl.when(t == 0)
    def _():
        xr_sc[0:SUBLANES, :] = jnp.zeros((SUBLANES, d), F32)
        hcar_sc[...] = jnp.zeros_like(hcar_sc)

    xn = _rms(x_ref[...], g_ref[...]).astype(BF16)
    xr_sc[SUBLANES:, :] = jnp.dot(xn, wxr_ref[...], preferred_element_type=F32)
    gg_sc[...] = _gelu_tanh(jnp.dot(xn, wgr_ref[...], preferred_element_type=F32)).astype(BF16)

    sub = lax.broadcasted_iota(jnp.int32, (SUBLANES, d), 0)
    cw = cw_ref[...]
    cb = cb_ref[...]

    def conv_body(gi, carry):
        r0 = pl.multiple_of(gi * SUBLANES, SUBLANES)
        prev = xr_sc[pl.ds(r0, SUBLANES), :]
        cur = xr_sc[pl.ds(r0 + SUBLANES, SUBLANES), :]
        out = cb + cur * cw[CONV_WIDTH - 1:CONV_WIDTH, :]
        for sh in range(1, CONV_WIDTH):
            shifted = jnp.where(sub >= sh, pltpu.roll(cur, sh, 0), pltpu.roll(prev, sh, 0))
            out = out + shifted * cw[CONV_WIDTH - 1 - sh:CONV_WIDTH - sh, :]
        xc_sc[pl.ds(r0, SUBLANES), :] = out
        return carry

    lax.fori_loop(0, ngroups, conv_body, 0)
    conv_ref[0] = xr_sc[tt:tt + SUBLANES, :]
    xr_sc[0:SUBLANES, :] = xr_sc[tt:tt + SUBLANES, :]

    _gate_matmul(xc_sc[...].astype(BF16), wg_ref, pre_sc)

    neg_c_sp = -LRU_C * _softplus(ap_ref[...])
    brg = brg_ref[...]
    big = big_ref[...]
    first_row = (sub == 0) & (t == 0)

    def lru_group(r0, is_first, hc):
        xc = xc_sc[pl.ds(r0, SUBLANES), :]
        pre = pre_sc[pl.ds(r0, SUBLANES), :]
        pre_r = jnp.concatenate([pre[:, 2 * j * MXU_DIM:(2 * j + 1) * MXU_DIM]
                                 for j in range(d // MXU_DIM)], axis=1)
        pre_i = jnp.concatenate([pre[:, (2 * j + 1) * MXU_DIM:(2 * j + 2) * MXU_DIM]
                                 for j in range(d // MXU_DIM)], axis=1)
        a, mult, ix = _lru_coeffs(pre_r, pre_i, xc, brg, big, neg_c_sp)
        if is_first is not None:
            mult = jnp.where(first_row & is_first, 1.0, mult)
        u = mult * ix
        for sh in (1, 2, 4):
            keep = sub >= sh
            a_s = jnp.where(keep, pltpu.roll(a, sh, 0), 1.0)
            u_s = jnp.where(keep, pltpu.roll(u, sh, 0), 0.0)
            u = u + a * u_s
            a = a * a_s
        h = u + a * hc
        return h, jnp.broadcast_to(h[SUBLANES - 1:SUBLANES, :], (SUBLANES, d))

    def lru_body(gi, hc):
        r0 = pl.multiple_of(gi * 2 * SUBLANES, 2 * SUBLANES)
        h0, hc = lru_group(r0, gi == 0, hc)
        h1, hc = lru_group(r0 + SUBLANES, None, hc)
        h = jnp.concatenate([h0, h1], axis=0)
        gg = gg_sc[pl.ds(r0, 2 * SUBLANES), :].astype(F32)
        rnn_ref[pl.ds(r0, 2 * SUBLANES), :] = (h * gg).astype(BF16)
        return hc

    hc = lax.fori_loop(0, ngroups // 2, lru_body, hcar_sc[...])
    hcar_sc[...] = hc
    h_ref[0] = hc


def _rnn_prompt(x, g, wxr, wgr, wg, brg, big, ap, cw, cb, b, s, tt):
    n, d = x.shape
    nt = s // tt
    vec = _const_spec((1, d))
    rnn, conv, h = pl.pallas_call(
        functools.partial(_rnn_prompt_kernel, tt=tt),
        grid=(b, nt),
        in_specs=[pl.BlockSpec((tt, d), lambda i, j: (i * nt + j, 0)), vec,
                  _const_spec(wxr.shape), _const_spec(wgr.shape), _const_spec(wg.shape),
                  vec, vec, vec, _const_spec(cw.shape), vec],
        out_specs=[pl.BlockSpec((tt, d), lambda i, j: (i * nt + j, 0)),
                   pl.BlockSpec((1, SUBLANES, d), lambda i, j: (i, 0, 0)),
                   pl.BlockSpec((1, SUBLANES, d), lambda i, j: (i, 0, 0))],
        out_shape=[jax.ShapeDtypeStruct((n, d), BF16),
                   jax.ShapeDtypeStruct((b, SUBLANES, d), F32),
                   jax.ShapeDtypeStruct((b, SUBLANES, d), F32)],
        scratch_shapes=[pltpu.VMEM((tt + SUBLANES, d), F32), pltpu.VMEM((tt, d), F32),
                        pltpu.VMEM((tt, 2 * d), F32), pltpu.VMEM((tt, d), BF16),
                        pltpu.VMEM((SUBLANES, d), F32)],
        compiler_params=_params(("arbitrary", "arbitrary")),
        name="rglru_prompt",
    )(x, g, wxr, wgr, wg, brg, big, ap, cw, cb)
    return rnn, conv[:, SUBLANES - (CONV_WIDTH - 1):, :], h[:, 0, :]


def _rnn_sample_kernel(x_ref, g_ref, wxr_ref, wgr_ref, wg_ref, brg_ref, big_ref, ap_ref, cw_ref,
                       cb_ref, sc_ref, h0_ref, rnn_ref, xr_ref, h_ref, pre_sc):
    d = x_ref.shape[1]
    xn = _rms(x_ref[...], g_ref[...]).astype(BF16)
    xr = jnp.dot(xn, wxr_ref[...], preferred_element_type=F32)
    gg = _gelu_tanh(jnp.dot(xn, wgr_ref[...], preferred_element_type=F32))
    cw = cw_ref[...]
    xc = cb_ref[...] + xr * cw[CONV_WIDTH - 1:CONV_WIDTH, :]
    for j in range(CONV_WIDTH - 1):
        xc = xc + sc_ref[j] * cw[j:j + 1, :]
    _gate_matmul(xc.astype(BF16), wg_ref, pre_sc)
    pre = pre_sc[...]
    pre_r = jnp.concatenate([pre[:, 2 * j * MXU_DIM:(2 * j + 1) * MXU_DIM]
                             for j in range(d // MXU_DIM)], axis=1)
    pre_i = jnp.concatenate([pre[:, (2 * j + 1) * MXU_DIM:(2 * j + 2) * MXU_DIM]
                             for j in range(d // MXU_DIM)], axis=1)
    a, mult, ix = _lru_coeffs(pre_r, pre_i, xc, brg_ref[...], big_ref[...],
                              -LRU_C * _softplus(ap_ref[...]))
    h = a * h0_ref[...] + mult * ix
    rnn_ref[...] = (h * gg).astype(BF16)
    xr_ref[...] = xr
    h_ref[...] = h


def _rnn_sample(x, g, wxr, wgr, wg, brg, big, ap, cw, cb, state_conv, h0):
    db, d = x.shape
    full = lambda shape: pl.BlockSpec(shape, lambda i: (0,) * len(shape))
    sc = jnp.swapaxes(state_conv, 0, 1)
    rnn, xr, h = pl.pallas_call(
        _rnn_sample_kernel,
        grid=(1,),
        in_specs=[full(a.shape) for a in (x, g, wxr, wgr, wg, brg, big, ap, cw, cb, sc, h0)],
        out_specs=[full((db, d))] * 3,
        out_shape=[jax.ShapeDtypeStruct((db, d), BF16), jax.ShapeDtypeStruct((db, d), F32),
                   jax.ShapeDtypeStruct((db, d), F32)],
        scratch_shapes=[pltpu.VMEM((db, 2 * d), F32)],
        compiler_params=_params(("arbitrary",)),
        name="rglru_sample",
    )(x, g, wxr, wgr, wg, brg, big, ap, cw, cb, sc, h0)
    conv_new = jnp.concatenate([state_conv[:, 1:, :], xr[:, None, :]], axis=1)
    return rnn, conv_new, h


def _post_kernel(x_ref, attn_ref, rnn_ref, ple_ref, g_pre_ref, wga_ref, wgrr_ref, wao_ref, wro_ref,
                 wout_ref, g_mixpost_ref, g_mlppre_ref, wff1_ref, wff2_ref, g_mlppost_ref,
                 g_plepre_ref, wpg_ref, wple_ref, g_plepost_ref, y_ref, *, ff_chunk):
    x = x_ref[...]
    xn = _rms(x, g_pre_ref[...]).astype(BF16)
    ga = _sigmoid(jnp.dot(xn, wga_ref[...], preferred_element_type=F32))
    m = ga * jnp.dot(attn_ref[...], wao_ref[...], preferred_element_type=F32)
    gr = _sigmoid(jnp.dot(xn, wgrr_ref[...], preferred_element_type=F32))
    m = m + gr * jnp.dot(rnn_ref[...], wro_ref[...], preferred_element_type=F32)
    x = x + _rms(jnp.dot(m.astype(BF16), wout_ref[...], preferred_element_type=F32),
                 g_mixpost_ref[...])
    hn = _rms(x, g_mlppre_ref[...]).astype(BF16)
    d_ff = wff1_ref.shape[1]
    f = None
    for c in range(d_ff // ff_chunk):
        lo = c * ff_chunk
        hid = jnp.dot(hn, wff1_ref[:, lo:lo + ff_chunk], preferred_element_type=F32)
        hid = jnp.square(jnp.maximum(hid, 0.0)).astype(BF16)
        part = jnp.dot(hid, wff2_ref[lo:lo + ff_chunk, :], preferred_element_type=F32)
        f = part if f is None else f + part
    x = x + _rms(f, g_mlppost_ref[...])
    gate = _sigmoid(jnp.dot(_rms(x, g_plepre_ref[...]).astype(BF16), wpg_ref[...],
                            preferred_element_type=F32))
    pe = jnp.dot(ple_ref[...].astype(BF16), wple_ref[...], preferred_element_type=F32)
    y_ref[...] = x + _rms(pe * gate, g_plepost_ref[...])


def _post(x, attn, rnn, ple, vecs, mats, tm):
    n, d = x.shape
    tok = lambda w: pl.BlockSpec((tm, w), lambda i: (i, 0))
    g_pre, g_mixpost, g_mlppre, g_mlppost, g_plepre, g_plepost = vecs
    wga, wgrr, wao, wro, wout, wff1, wff2, wpg, wple = mats
    vec = _const_spec((1, d))
    ops = (x, attn, rnn, ple, g_pre, wga, wgrr, wao, wro, wout, g_mixpost, g_mlppre, wff1, wff2,
           g_mlppost, g_plepre, wpg, wple, g_plepost)
    in_specs = [tok(d), tok(attn.shape[1]), tok(d), tok(ple.shape[1])]
    in_specs += [vec if a.shape == (1, d) else _const_spec(a.shape) for a in ops[4:]]
    return pl.pallas_call(
        functools.partial(_post_kernel, ff_chunk=min(1024, wff1.shape[1])),
        grid=(n // tm,),
        in_specs=in_specs,
        out_specs=tok(d),
        out_shape=jax.ShapeDtypeStruct((n, d), F32),
        compiler_params=_params(("arbitrary",)),
        name="merge_mlp_ple",
    )(*ops)


def _gate_blockdiag(w_rg, w_ig):
    nb = w_rg.shape[0]
    per = MXU_DIM // RNN_BLOCK
    eye = jnp.eye(per, dtype=F32)

    def bd(w):
        w = w.reshape(nb // per, per, RNN_BLOCK, RNN_BLOCK)
        t = jnp.einsum('jade,ab->jadbe', w, eye)
        return t.reshape(nb // per, MXU_DIM, MXU_DIM)

    return jnp.concatenate([bd(w_rg), bd(w_ig)], axis=2).astype(BF16)


def _pick(n, pref):
    t = min(n, pref)
    while n % t:
        t //= 2
    return t


def kernel(x_prompt, x_sample, p_prompt, p_sample, cache_k, cache_v, cache_logf, state_conv, state_h,
           page_table, g_mix_pre, w_in, b_f, w_rg, b_rg, w_ig, b_ig, a_param, conv_w, conv_b,
           w_attn_out, w_rnn_out, w_out, g_mix_post, g_mlp_pre, w_ff1, w_ff2, g_mlp_post,
           g_ple_pre, w_ple_gate, w_ple, g_ple_post):
    depth = w_in.shape[0]
    b, s, d = x_prompt.shape
    db = x_sample.shape[0]
    n_heads = b_f.shape[1]
    d_attn = n_heads * HEAD_DIM
    d_rnn = w_rg.shape[1] * RNN_BLOCK
    assert x_sample.shape[1] == 1 and d_rnn == d and n_heads == SUBLANES
    n_pool, page = cache_k.shape[1], cache_k.shape[2]

    xp = x_prompt.reshape(b * s, d)
    xs = x_sample.reshape(db, d)
    outs = [[] for _ in range(10)]
    for l in range(depth):
        row = lambda a: a[l].reshape(1, -1)
        wl = w_in[l]
        o = 3 * d_attn
        wqkv = wl[:, :o].astype(BF16)
        wf = jnp.pad(wl[:, o:o + n_heads], ((0, 0), (0, LANES - n_heads))).astype(BF16)
        bfp = jnp.pad(b_f[l].reshape(1, -1), ((0, 0), (0, LANES - n_heads)))
        o += n_heads
        wxr = wl[:, o:o + d_rnn].astype(BF16)
        wgr = wl[:, o + d_rnn:o + 2 * d_rnn].astype(BF16)
        o += 2 * d_rnn
        wga = wl[:, o:o + d].astype(BF16)
        wgrr = wl[:, o + d:o + 2 * d].astype(BF16)
        wg = _gate_blockdiag(w_rg[l], w_ig[l])
        rnn_w = (row(g_mix_pre), wxr, wgr, wg, row(b_rg), row(b_ig), row(a_param), conv_w[l],
                 row(conv_b))
        vecs = (row(g_mix_pre), row(g_mix_post), row(g_mlp_pre), row(g_mlp_post), row(g_ple_pre),
                row(g_ple_post))
        mats = (wga, wgrr, w_attn_out[l].astype(BF16), w_rnn_out[l].astype(BF16),
                w_out[l].astype(BF16), w_ff1[l].astype(BF16), w_ff2[l].astype(BF16),
                w_ple_gate[l].astype(BF16), w_ple[l].astype(BF16))

        tm = _pick(b * s, 512)
        q, k, v, kb, vb, lf, lft = _inproj(xp, row(g_mix_pre), wqkv, wf, bfp, tm)
        cc, cr = _cumsum(lf, lft, b, s, _pick(s, 512))
        attn = _attention(q, kb, vb, cc, cr, b, s, _pick(s, 512))
        rnn, conv_p, h_p = _rnn_prompt(xp, *rnn_w, b, s, _pick(s, 256))
        xp = _post(xp, attn, rnn, p_prompt[l].reshape(b * s, -1), vecs, mats, tm)
        for i, a in enumerate((k.reshape(b, s, n_heads, HEAD_DIM), v.reshape(b, s, n_heads, HEAD_DIM),
                               lf[:, :n_heads].reshape(b, s, n_heads), conv_p, h_p)):
            outs[i].append(a)

        qs, ks, vs, _, _, lfs, _ = _inproj(xs, row(g_mix_pre), wqkv, wf, bfp, db)
        attn_s = _sample_attention(
            qs, ks, vs, lfs[:, :n_heads].reshape(db, n_heads, 1),
            cache_k[l].reshape(n_pool, page, d_attn), cache_v[l].reshape(n_pool, page, d_attn),
            jnp.swapaxes(cache_logf[l], 1, 2), page_table, _pick(page_table.shape[1], 8))
        rnn_s, conv_s, h_s = _rnn_sample(xs, *rnn_w, state_conv[l], state_h[l].astype(F32))
        xs = _post(xs, attn_s, rnn_s, p_sample[l].reshape(db, -1), vecs, mats, db)
        for i, a in enumerate((ks.reshape(db, 1, n_heads, HEAD_DIM), vs.reshape(db, 1, n_heads, HEAD_DIM),
                               lfs[:, :n_heads].reshape(db, 1, n_heads), conv_s, h_s)):
            outs[5 + i].append(a)

    return (xp.reshape(b, s, d), xs.reshape(db, 1, d), *[jnp.stack(o) for o in outs])
```

```python
import functools
import math

import jax
import jax.numpy as jnp
from jax import lax
from jax.experimental import pallas as pl
from jax.experimental.pallas import tpu as pltpu

F32 = jnp.float32
BF16 = jnp.bfloat16
HIGHEST = lax.Precision.HIGHEST

RMS_EPS = 1e-6
LRU_C = 8.0
CONV_WIDTH = 4
RNN_BLOCK = 64
HEAD_DIM = 64
LANES = 128
SUBLANES = 8
MXU_DIM = 256
VMEM_LIMIT = 56 * 1024 * 1024
NEG_BIG = float(jnp.finfo(jnp.float32).min)
N_AUG = 3


def _rms(x, g):
    ms = jnp.mean(x * x, axis=-1, keepdims=True)
    return x * lax.rsqrt(ms + RMS_EPS) * g


def _softplus(x):
    return jnp.maximum(x, 0.0) + jnp.log1p(jnp.exp(-jnp.abs(x)))


def _sigmoid(x):
    return 0.5 * jnp.tanh(0.5 * x) + 0.5


def _gelu_tanh(x):
    c = math.sqrt(2.0 / math.pi)
    return 0.5 * x * (1.0 + jnp.tanh(c * (x + 0.044715 * (x * x * x))))


def _const_spec(shape):
    nd = len(shape)
    return pl.BlockSpec(shape, lambda *_: (0,) * nd, pipeline_mode=pl.Buffered(1))


def _params(sem):
    return pltpu.CompilerParams(dimension_semantics=sem, vmem_limit_bytes=VMEM_LIMIT)


def _inproj_kernel(x_ref, g_ref, wqkv_ref, wf_ref, bf_ref, q_ref, k_ref, v_ref, lf_ref, lft_ref,
                   *, d_attn):
    xn = _rms(x_ref[...], g_ref[...]).astype(BF16)
    qkv = jnp.dot(xn, wqkv_ref[...], preferred_element_type=F32)
    q_ref[...] = (qkv[:, :d_attn] * (1.0 / math.sqrt(HEAD_DIM))).astype(BF16)
    k_ref[...] = qkv[:, d_attn:2 * d_attn]
    v_ref[...] = qkv[:, 2 * d_attn:]
    f = jnp.dot(xn, wf_ref[...], preferred_element_type=F32) + bf_ref[...]
    lf = -_softplus(-f)
    lf_ref[...] = lf
    lft_ref[...] = lf.T[:SUBLANES, :]


def _inproj(x, g, wqkv, wf, bfp, tm):
    n, d = x.shape
    d_attn = wqkv.shape[1] // 3
    tok = lambda w: pl.BlockSpec((tm, w), lambda i: (i, 0))
    return pl.pallas_call(
        functools.partial(_inproj_kernel, d_attn=d_attn),
        grid=(n // tm,),
        in_specs=[tok(d), _const_spec((1, d)), _const_spec(wqkv.shape), _const_spec(wf.shape),
                  _const_spec((1, LANES))],
        out_specs=[tok(d_attn), tok(d_attn), tok(d_attn), tok(LANES),
                   pl.BlockSpec((SUBLANES, tm), lambda i: (0, i))],
        out_shape=[jax.ShapeDtypeStruct((n, d_attn), BF16),
                   jax.ShapeDtypeStruct((n, d_attn), F32),
                   jax.ShapeDtypeStruct((n, d_attn), F32),
                   jax.ShapeDtypeStruct((n, LANES), F32),
                   jax.ShapeDtypeStruct((SUBLANES, n), F32)],
        compiler_params=_params(("arbitrary",)),
        name="inproj",
    )(x, g, wqkv, wf, bfp)


def _head_lanes(lane, e):
    in_head = (lane >= HEAD_DIM * e) & (lane < HEAD_DIM * (e + 1))
    aug0 = HEAD_DIM * (1 - e)
    return in_head, aug0


def _prep_kernel(lf_ref, lft_ref, k_ref, v_ref, cr_ref, ka_ref, vt_ref, carry_c, carry_r,
                 *, tc, n_heads):
    @pl.when(pl.program_id(1) == 0)
    def _():
        carry_c[...] = jnp.zeros_like(carry_c)
        carry_r[...] = jnp.zeros_like(carry_r)

    row = lax.broadcasted_iota(jnp.int32, (tc, tc), 0)
    col = lax.broadcasted_iota(jnp.int32, (tc, tc), 1)
    lower = (col <= row).astype(F32)
    cc = jnp.dot(lower, lf_ref[...], precision=HIGHEST, preferred_element_type=F32) + carry_c[...]
    carry_c[...] = cc[tc - 1:tc, :]
    upper = (row <= col).astype(F32)
    cr = jnp.dot(lft_ref[...], upper, precision=HIGHEST, preferred_element_type=F32) + carry_r[...]
    cr_ref[...] = cr
    carry_r[...] = cr[:, tc - 1:tc]

    lane = lax.broadcasted_iota(jnp.int32, (tc, LANES), 1)
    for h in range(n_heads):
        e = h % 2
        in_head, aug0 = _head_lanes(lane, e)
        c = jnp.sum(jnp.where(lane == h, cc, 0.0), axis=1, keepdims=True)
        hi = c.astype(BF16).astype(F32)
        r1 = c - hi
        mid = r1.astype(BF16).astype(F32)
        lo = r1 - mid
        kp = k_ref[:, LANES * (h // 2):LANES * (h // 2 + 1)]
        aug = jnp.where(lane == aug0, hi,
                        jnp.where(lane == aug0 + 1, mid, jnp.where(lane == aug0 + 2, lo, 0.0)))
        ka_ref[:, LANES * h:LANES * (h + 1)] = jnp.where(in_head, kp, aug).astype(BF16)
        vp = v_ref[:, LANES * (h // 2):LANES * (h // 2 + 1)]
        vt_ref[LANES * h:LANES * (h + 1), :] = jnp.where(in_head, vp, 1.0).T.astype(BF16)


def _prep(lf, lft, k, v, b, s, tc):
    n, d_attn = k.shape
    n_heads = d_attn // HEAD_DIM
    nt = s // tc
    return pl.pallas_call(
        functools.partial(_prep_kernel, tc=tc, n_heads=n_heads),
        grid=(b, nt),
        in_specs=[pl.BlockSpec((tc, LANES), lambda i, j: (i * nt + j, 0)),
                  pl.BlockSpec((SUBLANES, tc), lambda i, j: (0, i * nt + j)),
                  pl.BlockSpec((tc, d_attn), lambda i, j: (i * nt + j, 0)),
                  pl.BlockSpec((tc, d_attn), lambda i, j: (i * nt + j, 0))],
        out_specs=[pl.BlockSpec((SUBLANES, tc), lambda i, j: (0, i * nt + j)),
                   pl.BlockSpec((tc, n_heads * LANES), lambda i, j: (i * nt + j, 0)),
                   pl.BlockSpec((n_heads * LANES, tc), lambda i, j: (0, i * nt + j))],
        out_shape=[jax.ShapeDtypeStruct((SUBLANES, n), F32),
                   jax.ShapeDtypeStruct((n, n_heads * LANES), BF16),
                   jax.ShapeDtypeStruct((n_heads * LANES, n), BF16)],
        scratch_shapes=[pltpu.VMEM((1, LANES), F32), pltpu.VMEM((SUBLANES, 1), F32)],
        compiler_params=_params(("arbitrary", "arbitrary")),
        name="attn_prep",
    )(lf, lft, k, v)


def _attn_kernel(q_ref, ka_ref, vt_ref, cr_ref, o_ref, qt_sc, m_sc, acc_sc, *, tq, tk):
    hp = pl.program_id(1)
    qi = pl.program_id(2)
    lane = lax.broadcasted_iota(jnp.int32, (tq, LANES), 1)
    q2 = q_ref[...].astype(F32)
    for e in range(2):
        in_head, aug0 = _head_lanes(lane, e)
        is_aug = (lane >= aug0) & (lane < aug0 + N_AUG)
        qa = jnp.where(in_head, q2, jnp.where(is_aug, -1.0, 0.0))
        qt_sc[e] = qa.T.astype(BF16)
        m_sc[e] = jnp.full((1, tq), NEG_BIG, F32)
        acc_sc[e] = jnp.zeros((LANES, tq), F32)

    def step(j, diag):
        k0 = pl.multiple_of(j * tk, tk)
        for e in range(2):
            ka = ka_ref[pl.ds(k0, tk), LANES * e:LANES * (e + 1)]
            st = jnp.dot(ka, qt_sc[e], preferred_element_type=F32)
            if diag is not None:
                kr = lax.broadcasted_iota(jnp.int32, (tk, tq), 0) + diag * tk
                qc = lax.broadcasted_iota(jnp.int32, (tk, tq), 1)
                st = jnp.where(kr <= qc, st, -jnp.inf)
            cq = cr_ref[pl.ds(2 * hp + e, 1), :]
            m_prev = m_sc[e]
            m_new = jnp.maximum(m_prev, cq + jnp.max(st, axis=0, keepdims=True))
            p = jnp.exp(st - (m_new - cq))
            alpha = jnp.exp(m_prev - m_new)
            vt = vt_ref[LANES * e:LANES * (e + 1), pl.ds(k0, tk)]
            acc_sc[e] = alpha * acc_sc[e] + jnp.dot(vt, p.astype(BF16), preferred_element_type=F32)
            m_sc[e] = m_new

    nfull = qi * (tq // tk)

    def body(j, carry):
        step(j, None)
        return carry

    lax.fori_loop(0, nfull, body, 0)
    for dd in range(tq // tk):
        step(nfull + dd, dd)
    a0 = acc_sc[0]
    a1 = acc_sc[1]
    row = lax.broadcasted_iota(jnp.int32, (LANES, tq), 0)
    o_t = jnp.where(row < HEAD_DIM, a0 / a0[HEAD_DIM:HEAD_DIM + 1, :], a1 / a1[0:1, :])
    o_ref[...] = o_t.T.astype(BF16)


def _attention(q, ka, vt, cr, b, s, tq, tk):
    n, d_attn = q.shape
    nq = s // tq
    npairs = d_attn // LANES
    return pl.pallas_call(
        functools.partial(_attn_kernel, tq=tq, tk=tk),
        grid=(b, npairs, nq),
        in_specs=[pl.BlockSpec((tq, LANES), lambda i, h, j: (i * nq + j, h)),
                  pl.BlockSpec((s, 2 * LANES), lambda i, h, j: (i, h)),
                  pl.BlockSpec((2 * LANES, s), lambda i, h, j: (h, i)),
                  pl.BlockSpec((SUBLANES, tq), lambda i, h, j: (0, i * nq + j))],
        out_specs=pl.BlockSpec((tq, LANES), lambda i, h, j: (i * nq + j, h)),
        out_shape=jax.ShapeDtypeStruct((n, d_attn), BF16),
        scratch_shapes=[pltpu.VMEM((2, LANES, tq), BF16), pltpu.VMEM((2, 1, tq), F32),
                        pltpu.VMEM((2, LANES, tq), F32)],
        compiler_params=_params(("arbitrary", "arbitrary", "arbitrary")),
        name="fox_attention",
    )(q, ka, vt, cr)


def _lane_roll_up(x, sh):
    return pltpu.roll(x, x.shape[-1] - sh, x.ndim - 1)


def _sample_attn_kernel(pt_ref, q_ref, kn_ref, vn_ref, lfn_ref, *rest, pages_per_step):
    del pt_ref
    pp = pages_per_step
    k_refs = rest[:pp]
    v_refs = rest[pp:2 * pp]
    lp_refs = rest[2 * pp:3 * pp]
    o_ref, m_sc, l_sc, acc_sc, carry_sc = rest[3 * pp:]
    g = pl.program_id(1)
    page, n_heads, hd = k_refs[0].shape[1:]
    rows = page * n_heads
    q8 = q_ref[0]
    q8b = q8.astype(BF16)

    @pl.when(g == 0)
    def _():
        kn = kn_ref[0].astype(BF16).astype(F32)
        m_sc[...] = jnp.sum(q8 * kn, axis=1, keepdims=True)
        l_sc[...] = jnp.ones_like(l_sc)
        acc_sc[...] = vn_ref[0].astype(BF16).astype(F32)
        carry_sc[...] = lfn_ref[0]

    sub = lax.broadcasted_iota(jnp.int32, (n_heads, rows), 0)
    lane_w = lax.broadcasted_iota(jnp.int32, (n_heads, rows), 1)
    own = (lane_w % n_heads) == sub
    lane = lax.broadcasted_iota(jnp.int32, (SUBLANES, LANES), 1)
    subl = lax.broadcasted_iota(jnp.int32, (SUBLANES, LANES), 0)
    carry = carry_sc[...]
    scores, values = [], []
    for i in range(pp):
        ka = k_refs[i][0].reshape(rows, hd).astype(BF16)
        values.append(v_refs[i][0].reshape(rows, hd).astype(BF16))
        lp = lp_refs[i][0]
        suf = lp
        tot = lp
        sh = n_heads
        while sh < LANES:
            suf = suf + jnp.where(lane + sh < LANES, _lane_roll_up(suf, sh), 0.0)
            tot = tot + _lane_roll_up(tot, sh)
            sh *= 2
        below = tot
        sh = 1
        while sh < SUBLANES:
            below = below + jnp.where(subl + sh < SUBLANES, pltpu.roll(below, SUBLANES - sh, 0), 0.0)
            sh *= 2
        bias = (suf - lp) + (below - tot) + carry
        carry = carry + below[0:1, :]
        bias_w = jnp.concatenate([jnp.broadcast_to(bias[r:r + 1, :], (n_heads, LANES))
                                  for r in range(SUBLANES)], axis=1)
        s = lax.dot_general(q8b, ka, (((1,), (1,)), ((), ())), preferred_element_type=F32) + bias_w
        scores.append(jnp.where(own, s, -jnp.inf))
    carry_sc[...] = carry
    smax = scores[0]
    for s in scores[1:]:
        smax = jnp.maximum(smax, s)
    m_prev = m_sc[...]
    m_new = jnp.maximum(m_prev, jnp.max(smax, axis=1, keepdims=True))
    alpha = jnp.exp(m_prev - m_new)
    psum = None
    pv = None
    for s, va in zip(scores, values):
        p = jnp.exp(s - m_new)
        psum = p if psum is None else psum + p
        d = jnp.dot(p.astype(BF16), va, preferred_element_type=F32)
        pv = d if pv is None else pv + d
    l_sc[...] = alpha * l_sc[...] + jnp.sum(psum, axis=1, keepdims=True)
    acc_sc[...] = alpha * acc_sc[...] + pv
    m_sc[...] = m_new

    @pl.when(g == pl.num_programs(1) - 1)
    def _():
        o_ref[0] = (acc_sc[...] / l_sc[...]).astype(BF16)


def _sample_attention(q8, kn8, vn8, lfn_lane, ck, cv, cl_flat, page_table, pages_per_step):
    db, n_heads, hd = q8.shape
    n_pages = page_table.shape[1]
    page = ck.shape[1]
    pp = pages_per_step
    steps = n_pages // pp
    pt = page_table.reshape(-1)

    def page_map(nd):
        def mk(i):
            return lambda b, g, pt_ref: (pt_ref[b * n_pages + (n_pages - 1 - (g * pp + i))],) + (0,) * nd
        return mk

    tok = lambda shape: pl.BlockSpec((1,) + shape, lambda b, g, pt_ref: (b,) + (0,) * len(shape))
    in_specs = [tok((n_heads, hd)), tok((n_heads, hd)), tok((n_heads, hd)), tok((1, LANES))]
    in_specs += [pl.BlockSpec((1, page, n_heads, hd), page_map(3)(i)) for i in range(pp)]
    in_specs += [pl.BlockSpec((1, page, n_heads, hd), page_map(3)(i)) for i in range(pp)]
    in_specs += [pl.BlockSpec((1, SUBLANES, LANES), page_map(2)(i)) for i in range(pp)]
    return pl.pallas_call(
        functools.partial(_sample_attn_kernel, pages_per_step=pp),
        grid_spec=pltpu.PrefetchScalarGridSpec(
            num_scalar_prefetch=1,
            grid=(db, steps),
            in_specs=in_specs,
            out_specs=tok((n_heads, hd)),
            scratch_shapes=[pltpu.VMEM((n_heads, 1), F32), pltpu.VMEM((n_heads, 1), F32),
                            pltpu.VMEM((n_heads, hd), F32), pltpu.VMEM((1, LANES), F32)]),
        out_shape=jax.ShapeDtypeStruct((db, n_heads, hd), BF16),
        compiler_params=_params(("arbitrary", "arbitrary")),
        name="sample_attention",
    )(pt, q8, kn8, vn8, lfn_lane, *([ck] * pp), *([cv] * pp), *([cl_flat] * pp))


def _lru_coeffs(pre_r, pre_i, xc, brg, big, neg_c_sp):
    r = _sigmoid(pre_r + brg)
    i = _sigmoid(pre_i + big)
    a = jnp.exp(r * neg_c_sp)
    mult = jnp.sqrt(1.0 - a * a)
    return a, mult, i * xc


def _gate_matmul(xcb, wg_ref, pre_ref):
    for j in range(wg_ref.shape[0]):
        lo = j * MXU_DIM
        pre_ref[:, 2 * lo:2 * lo + 2 * MXU_DIM] = jnp.dot(
            xcb[:, lo:lo + MXU_DIM], wg_ref[j], preferred_element_type=F32)


def _rnn_prompt_kernel(x_ref, g_ref, wxr_ref, wgr_ref, wg_ref, brg_ref, big_ref, ap_ref, cw_ref,
                       cb_ref, rnn_ref, conv_ref, h_ref,
                       xr_sc, xc_sc, pre_sc, gg_sc, hcar_sc, *, tt):
    t = pl.program_id(1)
    d = x_ref.shape[1]
    ngroups = tt // SUBLANES

    @pl.when(t == 0)
    def _():
        xr_sc[0:SUBLANES, :] = jnp.zeros((SUBLANES, d), F32)
        hcar_sc[...] = jnp.zeros_like(hcar_sc)

    xn = _rms(x_ref[...], g_ref[...]).astype(BF16)
    xr_sc[SUBLANES:, :] = jnp.dot(xn, wxr_ref[...], preferred_element_type=F32)
    gg_sc[...] = _gelu_tanh(jnp.dot(xn, wgr_ref[...], preferred_element_type=F32)).astype(BF16)

    sub = lax.broadcasted_iota(jnp.int32, (SUBLANES, d), 0)
    cw = cw_ref[...]
    cb = cb_ref[...]

    def conv_body(gi, carry):
        r0 = pl.multiple_of(gi * SUBLANES, SUBLANES)
        prev = xr_sc[pl.ds(r0, SUBLANES), :]
        cur = xr_sc[pl.ds(r0 + SUBLANES, SUBLANES), :]
        out = cb + cur * cw[CONV_WIDTH - 1:CONV_WIDTH, :]
        for sh in range(1, CONV_WIDTH):
            shifted = pltpu.roll(jnp.where(sub >= SUBLANES - sh, prev, cur), sh, 0)
            out = out + shifted * cw[CONV_WIDTH - 1 - sh:CONV_WIDTH - sh, :]
        xc_sc[pl.ds(r0, SUBLANES), :] = out
        return carry

    lax.fori_loop(0, ngroups, conv_body, 0)
    conv_ref[0] = xr_sc[tt:tt + SUBLANES, :]
    xr_sc[0:SUBLANES, :] = xr_sc[tt:tt + SUBLANES, :]

    _gate_matmul(xc_sc[...].astype(BF16), wg_ref, pre_sc)

    neg_c_sp = -LRU_C * _softplus(ap_ref[...])
    brg = brg_ref[...]
    big = big_ref[...]
    first_row = (sub == 0) & (t == 0)

    def lru_group(r0, is_first, hc):
        xc = xc_sc[pl.ds(r0, SUBLANES), :]
        pre = pre_sc[pl.ds(r0, SUBLANES), :]
        pre_r = jnp.concatenate([pre[:, 2 * j * MXU_DIM:(2 * j + 1) * MXU_DIM]
                                 for j in range(d // MXU_DIM)], axis=1)
        pre_i = jnp.concatenate([pre[:, (2 * j + 1) * MXU_DIM:(2 * j + 2) * MXU_DIM]
                                 for j in range(d // MXU_DIM)], axis=1)
        a, mult, ix = _lru_coeffs(pre_r, pre_i, xc, brg, big, neg_c_sp)
        if is_first is not None:
            mult = jnp.where(first_row & is_first, 1.0, mult)
        u = mult * ix
        for sh in (1, 2, 4):
            keep = sub >= sh
            a_s = jnp.where(keep, pltpu.roll(a, sh, 0), 1.0)
            u_s = jnp.where(keep, pltpu.roll(u, sh, 0), 0.0)
            u = u + a * u_s
            a = a * a_s
        h = u + a * hc
        return h, jnp.broadcast_to(h[SUBLANES - 1:SUBLANES, :], (SUBLANES, d))

    def lru_body(gi, hc):
        r0 = pl.multiple_of(gi * 2 * SUBLANES, 2 * SUBLANES)
        h0, hc = lru_group(r0, gi == 0, hc)
        h1, hc = lru_group(r0 + SUBLANES, None, hc)
        h = jnp.concatenate([h0, h1], axis=0)
        gg = gg_sc[pl.ds(r0, 2 * SUBLANES), :].astype(F32)
        rnn_ref[pl.ds(r0, 2 * SUBLANES), :] = (h * gg).astype(BF16)
        return hc

    hc = lax.fori_loop(0, ngroups // 2, lru_body, hcar_sc[...])
    hcar_sc[...] = hc
    h_ref[0] = hc


def _rnn_prompt(x, g, wxr, wgr, wg, brg, big, ap, cw, cb, b, s, tt):
    n, d = x.shape
    nt = s // tt
    vec = _const_spec((1, d))
    rnn, conv, h = pl.pallas_call(
        functools.partial(_rnn_prompt_kernel, tt=tt),
        grid=(b, nt),
        in_specs=[pl.BlockSpec((tt, d), lambda i, j: (i * nt + j, 0)), vec,
                  _const_spec(wxr.shape), _const_spec(wgr.shape), _const_spec(wg.shape),
                  vec, vec, vec, _const_spec(cw.shape), vec],
        out_specs=[pl.BlockSpec((tt, d), lambda i, j: (i * nt + j, 0)),
                   pl.BlockSpec((1, SUBLANES, d), lambda i, j: (i, 0, 0)),
                   pl.BlockSpec((1, SUBLANES, d), lambda i, j: (i, 0, 0))],
        out_shape=[jax.ShapeDtypeStruct((n, d), BF16),
                   jax.ShapeDtypeStruct((b, SUBLANES, d), F32),
                   jax.ShapeDtypeStruct((b, SUBLANES, d), F32)],
        scratch_shapes=[pltpu.VMEM((tt + SUBLANES, d), F32), pltpu.VMEM((tt, d), F32),
                        pltpu.VMEM((tt, 2 * d), F32), pltpu.VMEM((tt, d), BF16),
                        pltpu.VMEM((SUBLANES, d), F32)],
        compiler_params=_params(("arbitrary", "arbitrary")),
        name="rglru_prompt",
    )(x, g, wxr, wgr, wg, brg, big, ap, cw, cb)
    return rnn, conv[:, SUBLANES - (CONV_WIDTH - 1):, :], h[:, 0, :]


def _rnn_sample_kernel(x_ref, g_ref, wxr_ref, wgr_ref, wg_ref, brg_ref, big_ref, ap_ref, cw_ref,
                       cb_ref, sc_ref, h0_ref, rnn_ref, xr_ref, h_ref, pre_sc):
    d = x_ref.shape[1]
    xn = _rms(x_ref[...], g_ref[...]).astype(BF16)
    xr = jnp.dot(xn, wxr_ref[...], preferred_element_type=F32)
    gg = _gelu_tanh(jnp.dot(xn, wgr_ref[...], preferred_element_type=F32))
    cw = cw_ref[...]
    xc = cb_ref[...] + xr * cw[CONV_WIDTH - 1:CONV_WIDTH, :]
    for j in range(CONV_WIDTH - 1):
        xc = xc + sc_ref[j] * cw[j:j + 1, :]
    _gate_matmul(xc.astype(BF16), wg_ref, pre_sc)
    pre = pre_sc[...]
    pre_r = jnp.concatenate([pre[:, 2 * j * MXU_DIM:(2 * j + 1) * MXU_DIM]
                             for j in range(d // MXU_DIM)], axis=1)
    pre_i = jnp.concatenate([pre[:, (2 * j + 1) * MXU_DIM:(2 * j + 2) * MXU_DIM]
                             for j in range(d // MXU_DIM)], axis=1)
    a, mult, ix = _lru_coeffs(pre_r, pre_i, xc, brg_ref[...], big_ref[...],
                              -LRU_C * _softplus(ap_ref[...]))
    h = a * h0_ref[...] + mult * ix
    rnn_ref[...] = (h * gg).astype(BF16)
    xr_ref[...] = xr
    h_ref[...] = h


def _rnn_sample(x, g, wxr, wgr, wg, brg, big, ap, cw, cb, state_conv, h0):
    db, d = x.shape
    full = lambda shape: pl.BlockSpec(shape, lambda i: (0,) * len(shape))
    sc = jnp.swapaxes(state_conv, 0, 1)
    rnn, xr, h = pl.pallas_call(
        _rnn_sample_kernel,
        grid=(1,),
        in_specs=[full(a.shape) for a in (x, g, wxr, wgr, wg, brg, big, ap, cw, cb, sc, h0)],
        out_specs=[full((db, d))] * 3,
        out_shape=[jax.ShapeDtypeStruct((db, d), BF16), jax.ShapeDtypeStruct((db, d), F32),
                   jax.ShapeDtypeStruct((db, d), F32)],
        scratch_shapes=[pltpu.VMEM((db, 2 * d), F32)],
        compiler_params=_params(("arbitrary",)),
        name="rglru_sample",
    )(x, g, wxr, wgr, wg, brg, big, ap, cw, cb, sc, h0)
    conv_new = jnp.concatenate([state_conv[:, 1:, :], xr[:, None, :]], axis=1)
    return rnn, conv_new, h


def _post_kernel(x_ref, attn_ref, rnn_ref, ple_ref, g_pre_ref, wga_ref, wgrr_ref, wao_ref, wro_ref,
                 wout_ref, g_mixpost_ref, g_mlppre_ref, wff1_ref, wff2_ref, g_mlppost_ref,
                 g_plepre_ref, wpg_ref, wple_ref, g_plepost_ref, y_ref, *, ff_chunk):
    x = x_ref[...]
    xn = _rms(x, g_pre_ref[...]).astype(BF16)
    ga = _sigmoid(jnp.dot(xn, wga_ref[...], preferred_element_type=F32))
    m = ga * jnp.dot(attn_ref[...], wao_ref[...], preferred_element_type=F32)
    gr = _sigmoid(jnp.dot(xn, wgrr_ref[...], preferred_element_type=F32))
    m = m + gr * jnp.dot(rnn_ref[...], wro_ref[...], preferred_element_type=F32)
    x = x + _rms(jnp.dot(m.astype(BF16), wout_ref[...], preferred_element_type=F32),
                 g_mixpost_ref[...])
    hn = _rms(x, g_mlppre_ref[...]).astype(BF16)
    d_ff = wff1_ref.shape[1]
    f = None
    for c in range(d_ff // ff_chunk):
        lo = c * ff_chunk
        hid = jnp.dot(hn, wff1_ref[:, lo:lo + ff_chunk], preferred_element_type=F32)
        hid = jnp.square(jnp.maximum(hid, 0.0)).astype(BF16)
        part = jnp.dot(hid, wff2_ref[lo:lo + ff_chunk, :], preferred_element_type=F32)
        f = part if f is None else f + part
    x = x + _rms(f, g_mlppost_ref[...])
    gate = _sigmoid(jnp.dot(_rms(x, g_plepre_ref[...]).astype(BF16), wpg_ref[...],
                            preferred_element_type=F32))
    pe = jnp.dot(ple_ref[...].astype(BF16), wple_ref[...], preferred_element_type=F32)
    y_ref[...] = x + _rms(pe * gate, g_plepost_ref[...])


def _post(x, attn, rnn, ple, vecs, mats, tm):
    n, d = x.shape
    tok = lambda w: pl.BlockSpec((tm, w), lambda i: (i, 0))
    g_pre, g_mixpost, g_mlppre, g_mlppost, g_plepre, g_plepost = vecs
    wga, wgrr, wao, wro, wout, wff1, wff2, wpg, wple = mats
    vec = _const_spec((1, d))
    ops = (x, attn, rnn, ple, g_pre, wga, wgrr, wao, wro, wout, g_mixpost, g_mlppre, wff1, wff2,
           g_mlppost, g_plepre, wpg, wple, g_plepost)
    in_specs = [tok(d), tok(attn.shape[1]), tok(d), tok(ple.shape[1])]
    in_specs += [vec if a.shape == (1, d) else _const_spec(a.shape) for a in ops[4:]]
    return pl.pallas_call(
        functools.partial(_post_kernel, ff_chunk=min(1024, wff1.shape[1])),
        grid=(n // tm,),
        in_specs=in_specs,
        out_specs=tok(d),
        out_shape=jax.ShapeDtypeStruct((n, d), F32),
        compiler_params=_params(("arbitrary",)),
        name="merge_mlp_ple",
    )(*ops)


def _gate_blockdiag(w_rg, w_ig):
    nb = w_rg.shape[0]
    per = MXU_DIM // RNN_BLOCK
    eye = jnp.eye(per, dtype=F32)

    def bd(w):
        w = w.reshape(nb // per, per, RNN_BLOCK, RNN_BLOCK)
        t = jnp.einsum('jade,ab->jadbe', w, eye)
        return t.reshape(nb // per, MXU_DIM, MXU_DIM)

    return jnp.concatenate([bd(w_rg), bd(w_ig)], axis=2).astype(BF16)


def _pick(n, pref):
    t = min(n, pref)
    while n % t:
        t //= 2
    return t


def kernel(x_prompt, x_sample, p_prompt, p_sample, cache_k, cache_v, cache_logf, state_conv, state_h,
           page_table, g_mix_pre, w_in, b_f, w_rg, b_rg, w_ig, b_ig, a_param, conv_w, conv_b,
           w_attn_out, w_rnn_out, w_out, g_mix_post, g_mlp_pre, w_ff1, w_ff2, g_mlp_post,
           g_ple_pre, w_ple_gate, w_ple, g_ple_post):
    depth = w_in.shape[0]
    b, s, d = x_prompt.shape
    db = x_sample.shape[0]
    n_heads = b_f.shape[1]
    d_attn = n_heads * HEAD_DIM
    d_rnn = w_rg.shape[1] * RNN_BLOCK
    assert x_sample.shape[1] == 1 and d_rnn == d and n_heads == SUBLANES
    assert cache_k.shape[2] * n_heads == SUBLANES * LANES
    n_pool = cache_k.shape[1]

    xp = x_prompt.reshape(b * s, d)
    xs = x_sample.reshape(db, d)
    outs = [[] for _ in range(10)]
    for l in range(depth):
        row = lambda a: a[l].reshape(1, -1)
        wl = w_in[l]
        o = 3 * d_attn
        wqkv = wl[:, :o].astype(BF16)
        wf = jnp.pad(wl[:, o:o + n_heads], ((0, 0), (0, LANES - n_heads))).astype(BF16)
        bfp = jnp.pad(b_f[l].reshape(1, -1), ((0, 0), (0, LANES - n_heads)))
        o += n_heads
        wxr = wl[:, o:o + d_rnn].astype(BF16)
        wgr = wl[:, o + d_rnn:o + 2 * d_rnn].astype(BF16)
        o += 2 * d_rnn
        wga = wl[:, o:o + d].astype(BF16)
        wgrr = wl[:, o + d:o + 2 * d].astype(BF16)
        wg = _gate_blockdiag(w_rg[l], w_ig[l])
        rnn_w = (row(g_mix_pre), wxr, wgr, wg, row(b_rg), row(b_ig), row(a_param), conv_w[l],
                 row(conv_b))
        vecs = (row(g_mix_pre), row(g_mix_post), row(g_mlp_pre), row(g_mlp_post), row(g_ple_pre),
                row(g_ple_post))
        mats = (wga, wgrr, w_attn_out[l].astype(BF16), w_rnn_out[l].astype(BF16),
                w_out[l].astype(BF16), w_ff1[l].astype(BF16), w_ff2[l].astype(BF16),
                w_ple_gate[l].astype(BF16), w_ple[l].astype(BF16))

        tm = _pick(b * s, 512)
        q, k, v, lf, lft = _inproj(xp, row(g_mix_pre), wqkv, wf, bfp, tm)
        cr, ka, vt = _prep(lf, lft, k, v, b, s, _pick(s, 512))
        tk = _pick(s, 512)
        attn = _attention(q, ka, vt, cr, b, s, _pick(s, 2 * tk), tk)
        rnn, conv_p, h_p = _rnn_prompt(xp, *rnn_w, b, s, _pick(s, 256))
        xp = _post(xp, attn, rnn, p_prompt[l].reshape(b * s, -1), vecs, mats, tm)
        for i, a in enumerate((k.reshape(b, s, n_heads, HEAD_DIM), v.reshape(b, s, n_heads, HEAD_DIM),
                               lf[:, :n_heads].reshape(b, s, n_heads), conv_p, h_p)):
            outs[i].append(a)

        qs, ks, vs, lfs, _ = _inproj(xs, row(g_mix_pre), wqkv, wf, bfp, db)
        lfs = lfs[:, :n_heads]
        attn_s = _sample_attention(
            qs.astype(F32).reshape(db, n_heads, HEAD_DIM), ks.reshape(db, n_heads, HEAD_DIM),
            vs.reshape(db, n_heads, HEAD_DIM),
            jnp.tile(lfs, (1, LANES // n_heads)).reshape(db, 1, LANES),
            cache_k[l], cache_v[l], cache_logf[l].reshape(n_pool, SUBLANES, LANES), page_table,
            _pick(page_table.shape[1], 16)).reshape(db, d_attn)
        rnn_s, conv_s, h_s = _rnn_sample(xs, *rnn_w, state_conv[l], state_h[l].astype(F32))
        xs = _post(xs, attn_s, rnn_s, p_sample[l].reshape(db, -1), vecs, mats, db)
        for i, a in enumerate((ks.reshape(db, 1, n_heads, HEAD_DIM), vs.reshape(db, 1, n_heads, HEAD_DIM),
                               lfs.reshape(db, 1, n_heads), conv_s, h_s)):
            outs[5 + i].append(a)

    return (xp.reshape(b, s, d), xs.reshape(db, 1, d), *[jnp.stack(o) for o in outs])
```

```python
import functools
import math

import jax
import jax.numpy as jnp
from jax import lax
from jax.experimental import pallas as pl
from jax.experimental.pallas import tpu as pltpu

F32 = jnp.float32
BF16 = jnp.bfloat16
HIGHEST = lax.Precision.HIGHEST

RMS_EPS = 1e-6
LRU_C = 8.0
CONV_WIDTH = 4
RNN_BLOCK = 64
HEAD_DIM = 64
LANES = 128
SUBLANES = 8
MXU_DIM = 256
VMEM_LIMIT = 56 * 1024 * 1024
NEG_BIG = float(jnp.finfo(jnp.float32).min)
N_AUG = 3


def _rms(x, g):
    ms = jnp.mean(x * x, axis=-1, keepdims=True)
    return x * lax.rsqrt(ms + RMS_EPS) * g


def _softplus(x):
    return jnp.maximum(x, 0.0) + jnp.log1p(jnp.exp(-jnp.abs(x)))


def _sigmoid(x):
    return 0.5 * jnp.tanh(0.5 * x) + 0.5


def _gelu_tanh(x):
    c = math.sqrt(2.0 / math.pi)
    return 0.5 * x * (1.0 + jnp.tanh(c * (x + 0.044715 * (x * x * x))))


def _const_spec(shape):
    nd = len(shape)
    return pl.BlockSpec(shape, lambda *_: (0,) * nd, pipeline_mode=pl.Buffered(1))


def _params(sem):
    return pltpu.CompilerParams(dimension_semantics=sem, vmem_limit_bytes=VMEM_LIMIT)


def _inproj_kernel(x_ref, g_ref, wqkv_ref, wf_ref, bf_ref, q_ref, k_ref, *rest, d_attn, channel_major):
    lf_ref, lft_ref = rest[-2:]
    xn = _rms(x_ref[...], g_ref[...]).astype(BF16)
    qkv = jnp.dot(xn, wqkv_ref[...], preferred_element_type=F32)
    q_ref[...] = (qkv[:, :d_attn] * (1.0 / math.sqrt(HEAD_DIM))).astype(BF16)
    k = qkv[:, d_attn:2 * d_attn]
    v = qkv[:, 2 * d_attn:]
    k_ref[...] = k
    if channel_major:
        kt_ref, vt_ref = rest[:2]
        kt_ref[0] = k.T
        vt_ref[0] = v.T
    else:
        rest[0][...] = v
    f = jnp.dot(xn, wf_ref[...], preferred_element_type=F32) + bf_ref[...]
    lf = -_softplus(-f)
    lf_ref[...] = lf
    lft_ref[...] = lf.T[:SUBLANES, :]


def _inproj(x, g, wqkv, wf, bfp, tm, seqs=None):
    n, d = x.shape
    d_attn = wqkv.shape[1] // 3
    tok = lambda w: pl.BlockSpec((tm, w), lambda i: (i, 0))
    if seqs is None:
        kv_specs = [tok(d_attn)]
        kv_shapes = [jax.ShapeDtypeStruct((n, d_attn), F32)]
    else:
        b, s = seqs
        nt = s // tm
        kv_specs = [pl.BlockSpec((1, d_attn, tm), lambda i: (i // nt, 0, i % nt))] * 2
        kv_shapes = [jax.ShapeDtypeStruct((b, d_attn, s), F32)] * 2
    return pl.pallas_call(
        functools.partial(_inproj_kernel, d_attn=d_attn, channel_major=seqs is not None),
        grid=(n // tm,),
        in_specs=[tok(d), _const_spec((1, d)), _const_spec(wqkv.shape), _const_spec(wf.shape),
                  _const_spec((1, LANES))],
        out_specs=[tok(d_attn), tok(d_attn), *kv_specs, tok(LANES),
                   pl.BlockSpec((SUBLANES, tm), lambda i: (0, i))],
        out_shape=[jax.ShapeDtypeStruct((n, d_attn), BF16),
                   jax.ShapeDtypeStruct((n, d_attn), F32),
                   *kv_shapes,
                   jax.ShapeDtypeStruct((n, LANES), F32),
                   jax.ShapeDtypeStruct((SUBLANES, n), F32)],
        compiler_params=_params(("arbitrary",)),
        name="inproj",
    )(x, g, wqkv, wf, bfp)


def _head_lanes(lane, e):
    in_head = (lane >= HEAD_DIM * e) & (lane < HEAD_DIM * (e + 1))
    aug0 = HEAD_DIM * (1 - e)
    return in_head, aug0


def _prep_kernel(lf_ref, lft_ref, k_ref, vt_in_ref, cr_ref, ka_ref, vt_ref, carry_c, carry_r,
                 *, tc, n_heads):
    @pl.when(pl.program_id(1) == 0)
    def _():
        carry_c[...] = jnp.zeros_like(carry_c)
        carry_r[...] = jnp.zeros_like(carry_r)

    row = lax.broadcasted_iota(jnp.int32, (tc, tc), 0)
    col = lax.broadcasted_iota(jnp.int32, (tc, tc), 1)
    lower = (col <= row).astype(F32)
    cc = jnp.dot(lower, lf_ref[...], precision=HIGHEST, preferred_element_type=F32) + carry_c[...]
    carry_c[...] = cc[tc - 1:tc, :]
    upper = (row <= col).astype(F32)
    cr = jnp.dot(lft_ref[...], upper, precision=HIGHEST, preferred_element_type=F32) + carry_r[...]
    cr_ref[...] = cr
    carry_r[...] = cr[:, tc - 1:tc]

    lane = lax.broadcasted_iota(jnp.int32, (tc, LANES), 1)
    rowi = lax.broadcasted_iota(jnp.int32, (LANES, tc), 0)
    for h in range(n_heads):
        e = h % 2
        in_head, aug0 = _head_lanes(lane, e)
        c = jnp.sum(jnp.where(lane == h, cc, 0.0), axis=1, keepdims=True)
        hi = c.astype(BF16).astype(F32)
        r1 = c - hi
        mid = r1.astype(BF16).astype(F32)
        lo = r1 - mid
        kp = k_ref[:, LANES * (h // 2):LANES * (h // 2 + 1)]
        aug = jnp.where(lane == aug0, hi,
                        jnp.where(lane == aug0 + 1, mid, jnp.where(lane == aug0 + 2, lo, 0.0)))
        ka_ref[:, LANES * h:LANES * (h + 1)] = jnp.where(in_head, kp, aug).astype(BF16)
        vp = vt_in_ref[0, LANES * (h // 2):LANES * (h // 2 + 1), :]
        in_head_rows = (rowi >= HEAD_DIM * e) & (rowi < HEAD_DIM * (e + 1))
        vt_ref[LANES * h:LANES * (h + 1), :] = jnp.where(in_head_rows, vp, 1.0).astype(BF16)


def _prep(lf, lft, k, vt_in, b, s, tc):
    n, d_attn = k.shape
    n_heads = d_attn // HEAD_DIM
    nt = s // tc
    return pl.pallas_call(
        functools.partial(_prep_kernel, tc=tc, n_heads=n_heads),
        grid=(b, nt),
        in_specs=[pl.BlockSpec((tc, LANES), lambda i, j: (i * nt + j, 0)),
                  pl.BlockSpec((SUBLANES, tc), lambda i, j: (0, i * nt + j)),
                  pl.BlockSpec((tc, d_attn), lambda i, j: (i * nt + j, 0)),
                  pl.BlockSpec((1, d_attn, tc), lambda i, j: (i, 0, j))],
        out_specs=[pl.BlockSpec((SUBLANES, tc), lambda i, j: (0, i * nt + j)),
                   pl.BlockSpec((tc, n_heads * LANES), lambda i, j: (i * nt + j, 0)),
                   pl.BlockSpec((n_heads * LANES, tc), lambda i, j: (0, i * nt + j))],
        out_shape=[jax.ShapeDtypeStruct((SUBLANES, n), F32),
                   jax.ShapeDtypeStruct((n, n_heads * LANES), BF16),
                   jax.ShapeDtypeStruct((n_heads * LANES, n), BF16)],
        scratch_shapes=[pltpu.VMEM((1, LANES), F32), pltpu.VMEM((SUBLANES, 1), F32)],
        compiler_params=_params(("arbitrary", "arbitrary")),
        name="attn_prep",
    )(lf, lft, k, vt_in)


def _attn_kernel(q_ref, ka_ref, vt_ref, cr_ref, o_ref, qt_sc, m_sc, acc_sc, *, tq, tk):
    hp = pl.program_id(1)
    qi = pl.program_id(2)
    lane = lax.broadcasted_iota(jnp.int32, (tq, LANES), 1)
    q2 = q_ref[...].astype(F32)
    for e in range(2):
        in_head, aug0 = _head_lanes(lane, e)
        is_aug = (lane >= aug0) & (lane < aug0 + N_AUG)
        qa = jnp.where(in_head, q2, jnp.where(is_aug, -1.0, 0.0))
        qt_sc[e] = qa.T.astype(BF16)
        m_sc[e] = jnp.full((1, tq), NEG_BIG, F32)
        acc_sc[e] = jnp.zeros((LANES, tq), F32)

    def step(j, diag):
        k0 = pl.multiple_of(j * tk, tk)
        for e in range(2):
            ka = ka_ref[pl.ds(k0, tk), LANES * e:LANES * (e + 1)]
            st = jnp.dot(ka, qt_sc[e], preferred_element_type=F32)
            if diag is not None:
                kr = lax.broadcasted_iota(jnp.int32, (tk, tq), 0) + diag * tk
                qc = lax.broadcasted_iota(jnp.int32, (tk, tq), 1)
                st = jnp.where(kr <= qc, st, -jnp.inf)
            cq = cr_ref[pl.ds(2 * hp + e, 1), :]
            m_prev = m_sc[e]
            m_new = jnp.maximum(m_prev, cq + jnp.max(st, axis=0, keepdims=True))
            p = jnp.exp(st - (m_new - cq))
            alpha = jnp.exp(m_prev - m_new)
            vt = vt_ref[LANES * e:LANES * (e + 1), pl.ds(k0, tk)]
            acc_sc[e] = alpha * acc_sc[e] + jnp.dot(vt, p.astype(BF16), preferred_element_type=F32)
            m_sc[e] = m_new

    nfull = qi * (tq // tk)

    def body(j, carry):
        step(j, None)
        return carry

    lax.fori_loop(0, nfull, body, 0)
    for dd in range(tq // tk):
        step(nfull + dd, dd)
    a0 = acc_sc[0]
    a1 = acc_sc[1]
    row = lax.broadcasted_iota(jnp.int32, (LANES, tq), 0)
    o_t = jnp.where(row < HEAD_DIM, a0 / a0[HEAD_DIM:HEAD_DIM + 1, :], a1 / a1[0:1, :])
    o_ref[...] = o_t.T.astype(BF16)


def _attention(q, ka, vt, cr, b, s, tq, tk):
    n, d_attn = q.shape
    nq = s // tq
    npairs = d_attn // LANES
    return pl.pallas_call(
        functools.partial(_attn_kernel, tq=tq, tk=tk),
        grid=(b, npairs, nq),
        in_specs=[pl.BlockSpec((tq, LANES), lambda i, h, j: (i * nq + j, h)),
                  pl.BlockSpec((s, 2 * LANES), lambda i, h, j: (i, h)),
                  pl.BlockSpec((2 * LANES, s), lambda i, h, j: (h, i)),
                  pl.BlockSpec((SUBLANES, tq), lambda i, h, j: (0, i * nq + j))],
        out_specs=pl.BlockSpec((tq, LANES), lambda i, h, j: (i * nq + j, h)),
        out_shape=jax.ShapeDtypeStruct((n, d_attn), BF16),
        scratch_shapes=[pltpu.VMEM((2, LANES, tq), BF16), pltpu.VMEM((2, 1, tq), F32),
                        pltpu.VMEM((2, LANES, tq), F32)],
        compiler_params=_params(("arbitrary", "arbitrary", "arbitrary")),
        name="fox_attention",
    )(q, ka, vt, cr)


def _lane_roll_up(x, sh):
    return pltpu.roll(x, x.shape[-1] - sh, x.ndim - 1)


def _sample_attn_kernel(pt_ref, q_ref, kn_ref, vn_ref, lfn_ref, *rest, pages_per_step):
    del pt_ref
    pp = pages_per_step
    kt_refs = rest[:pp]
    vt_refs = rest[pp:2 * pp]
    lp_refs = rest[2 * pp:3 * pp]
    o_ref, m_sc, l_sc, acc_sc, carry_sc = rest[3 * pp:]
    g = pl.program_id(1)
    n_heads, hd, page = kt_refs[0].shape[1:]
    d_attn = n_heads * hd
    own = (lax.broadcasted_iota(jnp.int32, (n_heads, d_attn), 1) // hd
           == lax.broadcasted_iota(jnp.int32, (n_heads, d_attn), 0))
    qbd_f = jnp.where(own, jnp.broadcast_to(q_ref[0], (n_heads, d_attn)), 0.0)
    qbd = qbd_f.astype(BF16)

    @pl.when(g == 0)
    def _():
        kn = kn_ref[0].astype(BF16).astype(F32)
        m_sc[...] = jnp.sum(qbd_f * kn, axis=1, keepdims=True)
        l_sc[...] = jnp.ones_like(l_sc)
        acc_sc[...] = jnp.broadcast_to(vn_ref[0].astype(BF16).astype(F32), (n_heads, d_attn))
        carry_sc[...] = lfn_ref[0]

    lane = lax.broadcasted_iota(jnp.int32, (n_heads, page), 1)
    carry = carry_sc[...]
    scores, values = [], []
    for i in range(pp):
        kt = kt_refs[i][0].reshape(d_attn, page).astype(BF16)
        values.append(vt_refs[i][0].reshape(d_attn, page).astype(BF16))
        lp = lp_refs[i][0]
        suf = lp
        sh = 1
        while sh < page:
            suf = suf + jnp.where(lane + sh < page, _lane_roll_up(suf, sh), 0.0)
            sh *= 2
        scores.append(jnp.dot(qbd, kt, preferred_element_type=F32) + ((suf - lp) + carry))
        carry = carry + suf[:, 0:1]
    carry_sc[...] = carry
    smax = scores[0]
    for s in scores[1:]:
        smax = jnp.maximum(smax, s)
    m_prev = m_sc[...]
    m_new = jnp.maximum(m_prev, jnp.max(smax, axis=1, keepdims=True))
    alpha = jnp.exp(m_prev - m_new)
    psum = None
    pv = None
    for s, vt in zip(scores, values):
        p = jnp.exp(s - m_new)
        psum = p if psum is None else psum + p
        d = lax.dot_general(p.astype(BF16), vt, (((1,), (1,)), ((), ())), preferred_element_type=F32)
        pv = d if pv is None else pv + d
    l_sc[...] = alpha * l_sc[...] + jnp.sum(psum, axis=1, keepdims=True)
    acc_sc[...] = alpha * acc_sc[...] + pv
    m_sc[...] = m_new

    @pl.when(g == pl.num_programs(1) - 1)
    def _():
        o = jnp.where(own, acc_sc[...] / l_sc[...], 0.0)
        o_ref[0] = jnp.sum(o, axis=0, keepdims=True).astype(BF16)


def _sample_attention(q, kn, vn, lfn, ckt, cvt, clt, page_table, pages_per_step):
    db, d_attn = q.shape
    n_heads, hd, page = ckt.shape[1:]
    n_pages = page_table.shape[1]
    pp = pages_per_step
    steps = n_pages // pp
    pt = page_table.reshape(-1)

    def page_map(nd):
        def mk(i):
            return lambda b, g, pt_ref: (pt_ref[b * n_pages + (n_pages - 1 - (g * pp + i))],) + (0,) * nd
        return mk

    tok = lambda shape: pl.BlockSpec((1,) + shape, lambda b, g, pt_ref: (b,) + (0,) * len(shape))
    in_specs = [tok((1, d_attn)), tok((1, d_attn)), tok((1, d_attn)), tok((n_heads, 1))]
    in_specs += [pl.BlockSpec((1, n_heads, hd, page), page_map(3)(i)) for i in range(pp)]
    in_specs += [pl.BlockSpec((1, n_heads, hd, page), page_map(3)(i)) for i in range(pp)]
    in_specs += [pl.BlockSpec((1, n_heads, page), page_map(2)(i)) for i in range(pp)]
    out = pl.pallas_call(
        functools.partial(_sample_attn_kernel, pages_per_step=pp),
        grid_spec=pltpu.PrefetchScalarGridSpec(
            num_scalar_prefetch=1,
            grid=(db, steps),
            in_specs=in_specs,
            out_specs=tok((1, d_attn)),
            scratch_shapes=[pltpu.VMEM((n_heads, 1), F32), pltpu.VMEM((n_heads, 1), F32),
                            pltpu.VMEM((n_heads, d_attn), F32), pltpu.VMEM((n_heads, 1), F32)]),
        out_shape=jax.ShapeDtypeStruct((db, 1, d_attn), BF16),
        compiler_params=_params(("arbitrary", "arbitrary")),
        name="sample_attention",
    )(pt, q.reshape(db, 1, d_attn), kn.reshape(db, 1, d_attn), vn.reshape(db, 1, d_attn), lfn,
      *([ckt] * pp), *([cvt] * pp), *([clt] * pp))
    return out.reshape(db, d_attn)


def _lru_coeffs(pre_r, pre_i, xc, brg, big, neg_c_sp):
    r = _sigmoid(pre_r + brg)
    i = _sigmoid(pre_i + big)
    a = jnp.exp(r * neg_c_sp)
    mult = jnp.sqrt(1.0 - a * a)
    return a, mult, i * xc


def _gate_matmul(xcb, wg_ref, pre_ref):
    for j in range(wg_ref.shape[0]):
        lo = j * MXU_DIM
        pre_ref[:, 2 * lo:2 * lo + 2 * MXU_DIM] = jnp.dot(
            xcb[:, lo:lo + MXU_DIM], wg_ref[j], preferred_element_type=F32)


def _rnn_prompt_kernel(x_ref, g_ref, wxr_ref, wgr_ref, wg_ref, brg_ref, big_ref, ap_ref, cw_ref,
                       cb_ref, rnn_ref, conv_ref, h_ref,
                       xr_sc, xc_sc, pre_sc, gg_sc, hcar_sc, *, tt):
    t = pl.program_id(1)
    d = x_ref.shape[1]
    ngroups = tt // SUBLANES

    @pl.when(t == 0)
    def _():
        xr_sc[0:SUBLANES, :] = jnp.zeros((SUBLANES, d), F32)
        hcar_sc[...] = jnp.zeros_like(hcar_sc)

    xn = _rms(x_ref[...], g_ref[...]).astype(BF16)
    xr_sc[SUBLANES:, :] = jnp.dot(xn, wxr_ref[...], preferred_element_type=F32)
    gg_sc[...] = _gelu_tanh(jnp.dot(xn, wgr_ref[...], preferred_element_type=F32)).astype(BF16)

    sub = lax.broadcasted_iota(jnp.int32, (SUBLANES, d), 0)
    cw = cw_ref[...]
    cb = cb_ref[...]

    def conv_body(gi, carry):
        r0 = pl.multiple_of(gi * SUBLANES, SUBLANES)
        prev = xr_sc[pl.ds(r0, SUBLANES), :]
        cur = xr_sc[pl.ds(r0 + SUBLANES, SUBLANES), :]
        out = cb + cur * cw[CONV_WIDTH - 1:CONV_WIDTH, :]
        for sh in range(1, CONV_WIDTH):
            shifted = pltpu.roll(jnp.where(sub >= SUBLANES - sh, prev, cur), sh, 0)
            out = out + shifted * cw[CONV_WIDTH - 1 - sh:CONV_WIDTH - sh, :]
        xc_sc[pl.ds(r0, SUBLANES), :] = out
        return carry

    lax.fori_loop(0, ngroups, conv_body, 0)
    conv_ref[0] = xr_sc[tt:tt + SUBLANES, :]
    xr_sc[0:SUBLANES, :] = xr_sc[tt:tt + SUBLANES, :]

    _gate_matmul(xc_sc[...].astype(BF16), wg_ref, pre_sc)

    neg_c_sp = -LRU_C * _softplus(ap_ref[...])
    brg = brg_ref[...]
    big = big_ref[...]
    first_row = (sub == 0) & (t == 0)

    def lru_group(r0, is_first, hc):
        xc = xc_sc[pl.ds(r0, SUBLANES), :]
        pre = pre_sc[pl.ds(r0, SUBLANES), :]
        pre_r = jnp.concatenate([pre[:, 2 * j * MXU_DIM:(2 * j + 1) * MXU_DIM]
                                 for j in range(d // MXU_DIM)], axis=1)
        pre_i = jnp.concatenate([pre[:, (2 * j + 1) * MXU_DIM:(2 * j + 2) * MXU_DIM]
                                 for j in range(d // MXU_DIM)], axis=1)
        a, mult, ix = _lru_coeffs(pre_r, pre_i, xc, brg, big, neg_c_sp)
        if is_first is not None:
            mult = jnp.where(first_row & is_first, 1.0, mult)
        u = mult * ix
        for sh in (1, 2, 4):
            keep = sub >= sh
            a_s = jnp.where(keep, pltpu.roll(a, sh, 0), 1.0)
            u_s = jnp.where(keep, pltpu.roll(u, sh, 0), 0.0)
            u = u + a * u_s
            a = a * a_s
        h = u + a * hc
        return h, jnp.broadcast_to(h[SUBLANES - 1:SUBLANES, :], (SUBLANES, d))

    def lru_body(gi, hc):
        r0 = pl.multiple_of(gi * 2 * SUBLANES, 2 * SUBLANES)
        h0, hc = lru_group(r0, gi == 0, hc)
        h1, hc = lru_group(r0 + SUBLANES, None, hc)
        h = jnp.concatenate([h0, h1], axis=0)
        gg = gg_sc[pl.ds(r0, 2 * SUBLANES), :].astype(F32)
        rnn_ref[pl.ds(r0, 2 * SUBLANES), :] = (h * gg).astype(BF16)
        return hc

    hc = lax.fori_loop(0, ngroups // 2, lru_body, hcar_sc[...])
    hcar_sc[...] = hc
    h_ref[0] = hc


def _rnn_prompt(x, g, wxr, wgr, wg, brg, big, ap, cw, cb, b, s, tt):
    n, d = x.shape
    nt = s // tt
    vec = _const_spec((1, d))
    rnn, conv, h = pl.pallas_call(
        functools.partial(_rnn_prompt_kernel, tt=tt),
        grid=(b, nt),
        in_specs=[pl.BlockSpec((tt, d), lambda i, j: (i * nt + j, 0)), vec,
                  _const_spec(wxr.shape), _const_spec(wgr.shape), _const_spec(wg.shape),
                  vec, vec, vec, _const_spec(cw.shape), vec],
        out_specs=[pl.BlockSpec((tt, d), lambda i, j: (i * nt + j, 0)),
                   pl.BlockSpec((1, SUBLANES, d), lambda i, j: (i, 0, 0)),
                   pl.BlockSpec((1, SUBLANES, d), lambda i, j: (i, 0, 0))],
        out_shape=[jax.ShapeDtypeStruct((n, d), BF16),
                   jax.ShapeDtypeStruct((b, SUBLANES, d), F32),
                   jax.ShapeDtypeStruct((b, SUBLANES, d), F32)],
        scratch_shapes=[pltpu.VMEM((tt + SUBLANES, d), F32), pltpu.VMEM((tt, d), F32),
                        pltpu.VMEM((tt, 2 * d), F32), pltpu.VMEM((tt, d), BF16),
                        pltpu.VMEM((SUBLANES, d), F32)],
        compiler_params=_params(("arbitrary", "arbitrary")),
        name="rglru_prompt",
    )(x, g, wxr, wgr, wg, brg, big, ap, cw, cb)
    return rnn, conv[:, SUBLANES - (CONV_WIDTH - 1):, :], h[:, 0, :]


def _rnn_sample_kernel(x_ref, g_ref, wxr_ref, wgr_ref, wg_ref, brg_ref, big_ref, ap_ref, cw_ref,
                       cb_ref, sc_ref, h0_ref, rnn_ref, xr_ref, h_ref, pre_sc):
    d = x_ref.shape[1]
    xn = _rms(x_ref[...], g_ref[...]).astype(BF16)
    xr = jnp.dot(xn, wxr_ref[...], preferred_element_type=F32)
    gg = _gelu_tanh(jnp.dot(xn, wgr_ref[...], preferred_element_type=F32))
    cw = cw_ref[...]
    xc = cb_ref[...] + xr * cw[CONV_WIDTH - 1:CONV_WIDTH, :]
    for j in range(CONV_WIDTH - 1):
        xc = xc + sc_ref[j] * cw[j:j + 1, :]
    _gate_matmul(xc.astype(BF16), wg_ref, pre_sc)
    pre = pre_sc[...]
    pre_r = jnp.concatenate([pre[:, 2 * j * MXU_DIM:(2 * j + 1) * MXU_DIM]
                             for j in range(d // MXU_DIM)], axis=1)
    pre_i = jnp.concatenate([pre[:, (2 * j + 1) * MXU_DIM:(2 * j + 2) * MXU_DIM]
                             for j in range(d // MXU_DIM)], axis=1)
    a, mult, ix = _lru_coeffs(pre_r, pre_i, xc, brg_ref[...], big_ref[...],
                              -LRU_C * _softplus(ap_ref[...]))
    h = a * h0_ref[...] + mult * ix
    rnn_ref[...] = (h * gg).astype(BF16)
    xr_ref[...] = xr
    h_ref[...] = h


def _rnn_sample(x, g, wxr, wgr, wg, brg, big, ap, cw, cb, state_conv, h0):
    db, d = x.shape
    full = lambda shape: pl.BlockSpec(shape, lambda i: (0,) * len(shape))
    sc = jnp.swapaxes(state_conv, 0, 1)
    rnn, xr, h = pl.pallas_call(
        _rnn_sample_kernel,
        grid=(1,),
        in_specs=[full(a.shape) for a in (x, g, wxr, wgr, wg, brg, big, ap, cw, cb, sc, h0)],
        out_specs=[full((db, d))] * 3,
        out_shape=[jax.ShapeDtypeStruct((db, d), BF16), jax.ShapeDtypeStruct((db, d), F32),
                   jax.ShapeDtypeStruct((db, d), F32)],
        scratch_shapes=[pltpu.VMEM((db, 2 * d), F32)],
        compiler_params=_params(("arbitrary",)),
        name="rglru_sample",
    )(x, g, wxr, wgr, wg, brg, big, ap, cw, cb, sc, h0)
    conv_new = jnp.concatenate([state_conv[:, 1:, :], xr[:, None, :]], axis=1)
    return rnn, conv_new, h


def _post_kernel(x_ref, attn_ref, rnn_ref, ple_ref, g_pre_ref, wga_ref, wgrr_ref, wao_ref, wro_ref,
                 wout_ref, g_mixpost_ref, g_mlppre_ref, wff1_ref, wff2_ref, g_mlppost_ref,
                 g_plepre_ref, wpg_ref, wple_ref, g_plepost_ref, y_ref, *, ff_chunk):
    x = x_ref[...]
    xn = _rms(x, g_pre_ref[...]).astype(BF16)
    ga = _sigmoid(jnp.dot(xn, wga_ref[...], preferred_element_type=F32))
    m = ga * jnp.dot(attn_ref[...], wao_ref[...], preferred_element_type=F32)
    gr = _sigmoid(jnp.dot(xn, wgrr_ref[...], preferred_element_type=F32))
    m = m + gr * jnp.dot(rnn_ref[...], wro_ref[...], preferred_element_type=F32)
    x = x + _rms(jnp.dot(m.astype(BF16), wout_ref[...], preferred_element_type=F32),
                 g_mixpost_ref[...])
    hn = _rms(x, g_mlppre_ref[...]).astype(BF16)
    d_ff = wff1_ref.shape[1]
    f = None
    for c in range(d_ff // ff_chunk):
        lo = c * ff_chunk
        hid = jnp.dot(hn, wff1_ref[:, lo:lo + ff_chunk], preferred_element_type=F32)
        hid = jnp.square(jnp.maximum(hid, 0.0)).astype(BF16)
        part = jnp.dot(hid, wff2_ref[lo:lo + ff_chunk, :], preferred_element_type=F32)
        f = part if f is None else f + part
    x = x + _rms(f, g_mlppost_ref[...])
    gate = _sigmoid(jnp.dot(_rms(x, g_plepre_ref[...]).astype(BF16), wpg_ref[...],
                            preferred_element_type=F32))
    pe = jnp.dot(ple_ref[...].astype(BF16), wple_ref[...], preferred_element_type=F32)
    y_ref[...] = x + _rms(pe * gate, g_plepost_ref[...])


def _post(x, attn, rnn, ple, vecs, mats, tm):
    n, d = x.shape
    tok = lambda w: pl.BlockSpec((tm, w), lambda i: (i, 0))
    g_pre, g_mixpost, g_mlppre, g_mlppost, g_plepre, g_plepost = vecs
    wga, wgrr, wao, wro, wout, wff1, wff2, wpg, wple = mats
    vec = _const_spec((1, d))
    ops = (x, attn, rnn, ple, g_pre, wga, wgrr, wao, wro, wout, g_mixpost, g_mlppre, wff1, wff2,
           g_mlppost, g_plepre, wpg, wple, g_plepost)
    in_specs = [tok(d), tok(attn.shape[1]), tok(d), tok(ple.shape[1])]
    in_specs += [vec if a.shape == (1, d) else _const_spec(a.shape) for a in ops[4:]]
    return pl.pallas_call(
        functools.partial(_post_kernel, ff_chunk=min(1024, wff1.shape[1])),
        grid=(n // tm,),
        in_specs=in_specs,
        out_specs=tok(d),
        out_shape=jax.ShapeDtypeStruct((n, d), F32),
        compiler_params=_params(("arbitrary",)),
        name="merge_mlp_ple",
    )(*ops)


def _gate_blockdiag(w_rg, w_ig):
    nb = w_rg.shape[0]
    per = MXU_DIM // RNN_BLOCK
    eye = jnp.eye(per, dtype=F32)

    def bd(w):
        w = w.reshape(nb // per, per, RNN_BLOCK, RNN_BLOCK)
        t = jnp.einsum('jade,ab->jadbe', w, eye)
        return t.reshape(nb // per, MXU_DIM, MXU_DIM)

    return jnp.concatenate([bd(w_rg), bd(w_ig)], axis=2).astype(BF16)


def _pick(n, pref):
    t = min(n, pref)
    while n % t:
        t //= 2
    return t


def kernel(x_prompt, x_sample, p_prompt, p_sample, cache_k, cache_v, cache_logf, state_conv, state_h,
           page_table, g_mix_pre, w_in, b_f, w_rg, b_rg, w_ig, b_ig, a_param, conv_w, conv_b,
           w_attn_out, w_rnn_out, w_out, g_mix_post, g_mlp_pre, w_ff1, w_ff2, g_mlp_post,
           g_ple_pre, w_ple_gate, w_ple, g_ple_post):
    depth = w_in.shape[0]
    b, s, d = x_prompt.shape
    db = x_sample.shape[0]
    n_heads = b_f.shape[1]
    d_attn = n_heads * HEAD_DIM
    d_rnn = w_rg.shape[1] * RNN_BLOCK
    assert x_sample.shape[1] == 1 and d_rnn == d and n_heads == SUBLANES

    xp = x_prompt.reshape(b * s, d)
    xs = x_sample.reshape(db, d)
    outs = [[] for _ in range(10)]
    for l in range(depth):
        row = lambda a: a[l].reshape(1, -1)
        wl = w_in[l]
        o = 3 * d_attn
        wqkv = wl[:, :o].astype(BF16)
        wf = jnp.pad(wl[:, o:o + n_heads], ((0, 0), (0, LANES - n_heads))).astype(BF16)
        bfp = jnp.pad(b_f[l].reshape(1, -1), ((0, 0), (0, LANES - n_heads)))
        o += n_heads
        wxr = wl[:, o:o + d_rnn].astype(BF16)
        wgr = wl[:, o + d_rnn:o + 2 * d_rnn].astype(BF16)
        o += 2 * d_rnn
        wga = wl[:, o:o + d].astype(BF16)
        wgrr = wl[:, o + d:o + 2 * d].astype(BF16)
        wg = _gate_blockdiag(w_rg[l], w_ig[l])
        rnn_w = (row(g_mix_pre), wxr, wgr, wg, row(b_rg), row(b_ig), row(a_param), conv_w[l],
                 row(conv_b))
        vecs = (row(g_mix_pre), row(g_mix_post), row(g_mlp_pre), row(g_mlp_post), row(g_ple_pre),
                row(g_ple_post))
        mats = (wga, wgrr, w_attn_out[l].astype(BF16), w_rnn_out[l].astype(BF16),
                w_out[l].astype(BF16), w_ff1[l].astype(BF16), w_ff2[l].astype(BF16),
                w_ple_gate[l].astype(BF16), w_ple[l].astype(BF16))

        tm = _pick(s, 512)
        q, k, k_cm, v_cm, lf, lft = _inproj(xp, row(g_mix_pre), wqkv, wf, bfp, tm, (b, s))
        cr, ka, vt = _prep(lf, lft, k, v_cm, b, s, _pick(s, 512))
        tk = _pick(s, 512)
        attn = _attention(q, ka, vt, cr, b, s, _pick(s, 2 * tk), tk)
        rnn, conv_p, h_p = _rnn_prompt(xp, *rnn_w, b, s, _pick(s, 256))
        xp = _post(xp, attn, rnn, p_prompt[l].reshape(b * s, -1), vecs, mats, tm)
        seq_major = lambda a: jnp.transpose(a.reshape(b, n_heads, HEAD_DIM, s), (0, 3, 1, 2))
        for i, a in enumerate((seq_major(k_cm), seq_major(v_cm),
                               lf[:, :n_heads].reshape(b, s, n_heads), conv_p, h_p)):
            outs[i].append(a)

        qs, ks, vs, lfs, _ = _inproj(xs, row(g_mix_pre), wqkv, wf, bfp, db)
        lfs = lfs[:, :n_heads]
        attn_s = _sample_attention(
            qs.astype(F32), ks, vs, lfs.reshape(db, n_heads, 1),
            jnp.transpose(cache_k[l], (0, 2, 3, 1)), jnp.transpose(cache_v[l], (0, 2, 3, 1)),
            jnp.transpose(cache_logf[l], (0, 2, 1)), page_table, _pick(page_table.shape[1], 32))
        rnn_s, conv_s, h_s = _rnn_sample(xs, *rnn_w, state_conv[l], state_h[l].astype(F32))
        xs = _post(xs, attn_s, rnn_s, p_sample[l].reshape(db, -1), vecs, mats, db)
        for i, a in enumerate((ks.reshape(db, 1, n_heads, HEAD_DIM), vs.reshape(db, 1, n_heads, HEAD_DIM),
                               lfs.reshape(db, 1, n_heads), conv_s, h_s)):
            outs[5 + i].append(a)

    return (xp.reshape(b, s, d), xs.reshape(db, 1, d), *[jnp.stack(o) for o in outs])
```

```python
import functools
import math

import jax
import jax.numpy as jnp
from jax import lax
from jax.experimental import pallas as pl
from jax.experimental.pallas import tpu as pltpu

F32 = jnp.float32
BF16 = jnp.bfloat16
HIGHEST = lax.Precision.HIGHEST

RMS_EPS = 1e-6
LRU_C = 8.0
CONV_WIDTH = 4
RNN_BLOCK = 64
HEAD_DIM = 64
LANES = 128
SUBLANES = 8
MXU_DIM = 256
VMEM_LIMIT = 56 * 1024 * 1024
NEG_BIG = float(jnp.finfo(jnp.float32).min)
N_AUG = 3


def _rms(x, g):
    ms = jnp.mean(x * x, axis=-1, keepdims=True)
    return x * lax.rsqrt(ms + RMS_EPS) * g


def _softplus(x):
    return jnp.maximum(x, 0.0) + jnp.log1p(jnp.exp(-jnp.abs(x)))


def _sigmoid(x):
    return 0.5 * jnp.tanh(0.5 * x) + 0.5


def _gelu_tanh(x):
    c = math.sqrt(2.0 / math.pi)
    return 0.5 * x * (1.0 + jnp.tanh(c * (x + 0.044715 * (x * x * x))))


def _const_spec(shape):
    nd = len(shape)
    return pl.BlockSpec(shape, lambda *_: (0,) * nd, pipeline_mode=pl.Buffered(1))


def _params(sem):
    return pltpu.CompilerParams(dimension_semantics=sem, vmem_limit_bytes=VMEM_LIMIT)


def _inproj_kernel(x_ref, g_ref, wqkv_ref, wf_ref, bf_ref, q_ref, k_ref, *rest, d_attn, channel_major):
    lf_ref, lft_ref = rest[-2:]
    xn = _rms(x_ref[...], g_ref[...]).astype(BF16)
    qkv = jnp.dot(xn, wqkv_ref[...], preferred_element_type=F32)
    q_ref[...] = (qkv[:, :d_attn] * (1.0 / math.sqrt(HEAD_DIM))).astype(BF16)
    k = qkv[:, d_attn:2 * d_attn]
    v = qkv[:, 2 * d_attn:]
    k_ref[...] = k
    if channel_major:
        kt_ref, vt_ref = rest[:2]
        kt_ref[0] = k.T
        vt_ref[0] = v.T
    else:
        rest[0][...] = v
    f = jnp.dot(xn, wf_ref[...], preferred_element_type=F32) + bf_ref[...]
    lf = -_softplus(-f)
    lf_ref[...] = lf
    lft_ref[...] = lf.T[:SUBLANES, :]


def _inproj(x, g, wqkv, wf, bfp, tm, seqs=None):
    n, d = x.shape
    d_attn = wqkv.shape[1] // 3
    tok = lambda w: pl.BlockSpec((tm, w), lambda i: (i, 0))
    if seqs is None:
        kv_specs = [tok(d_attn)]
        kv_shapes = [jax.ShapeDtypeStruct((n, d_attn), F32)]
    else:
        b, s = seqs
        nt = s // tm
        kv_specs = [pl.BlockSpec((1, d_attn, tm), lambda i: (i // nt, 0, i % nt))] * 2
        kv_shapes = [jax.ShapeDtypeStruct((b, d_attn, s), F32)] * 2
    return pl.pallas_call(
        functools.partial(_inproj_kernel, d_attn=d_attn, channel_major=seqs is not None),
        grid=(n // tm,),
        in_specs=[tok(d), _const_spec((1, d)), _const_spec(wqkv.shape), _const_spec(wf.shape),
                  _const_spec((1, LANES))],
        out_specs=[tok(d_attn), tok(d_attn), *kv_specs, tok(LANES),
                   pl.BlockSpec((SUBLANES, tm), lambda i: (0, i))],
        out_shape=[jax.ShapeDtypeStruct((n, d_attn), BF16),
                   jax.ShapeDtypeStruct((n, d_attn), F32),
                   *kv_shapes,
                   jax.ShapeDtypeStruct((n, LANES), F32),
                   jax.ShapeDtypeStruct((SUBLANES, n), F32)],
        compiler_params=_params(("arbitrary",)),
        name="inproj",
    )(x, g, wqkv, wf, bfp)


def _head_lanes(lane, e):
    in_head = (lane >= HEAD_DIM * e) & (lane < HEAD_DIM * (e + 1))
    aug0 = HEAD_DIM * (1 - e)
    return in_head, aug0


def _prep_kernel(lf_ref, lft_ref, k_ref, vt_in_ref, cr_ref, ka_ref, vt_ref, carry_c, carry_r,
                 *, tc, n_heads):
    @pl.when(pl.program_id(1) == 0)
    def _():
        carry_c[...] = jnp.zeros_like(carry_c)
        carry_r[...] = jnp.zeros_like(carry_r)

    row = lax.broadcasted_iota(jnp.int32, (tc, tc), 0)
    col = lax.broadcasted_iota(jnp.int32, (tc, tc), 1)
    lower = (col <= row).astype(F32)
    cc = jnp.dot(lower, lf_ref[...], precision=HIGHEST, preferred_element_type=F32) + carry_c[...]
    carry_c[...] = cc[tc - 1:tc, :]
    upper = (row <= col).astype(F32)
    cr = jnp.dot(lft_ref[...], upper, precision=HIGHEST, preferred_element_type=F32) + carry_r[...]
    cr_ref[...] = cr
    carry_r[...] = cr[:, tc - 1:tc]

    lane = lax.broadcasted_iota(jnp.int32, (tc, LANES), 1)
    rowi = lax.broadcasted_iota(jnp.int32, (LANES, tc), 0)
    for h in range(n_heads):
        e = h % 2
        in_head, aug0 = _head_lanes(lane, e)
        c = jnp.sum(jnp.where(lane == h, cc, 0.0), axis=1, keepdims=True)
        hi = c.astype(BF16).astype(F32)
        r1 = c - hi
        mid = r1.astype(BF16).astype(F32)
        lo = r1 - mid
        kp = k_ref[:, LANES * (h // 2):LANES * (h // 2 + 1)]
        aug = jnp.where(lane == aug0, hi,
                        jnp.where(lane == aug0 + 1, mid, jnp.where(lane == aug0 + 2, lo, 0.0)))
        ka_ref[:, LANES * h:LANES * (h + 1)] = jnp.where(in_head, kp, aug).astype(BF16)
        vp = vt_in_ref[0, LANES * (h // 2):LANES * (h // 2 + 1), :]
        in_head_rows = (rowi >= HEAD_DIM * e) & (rowi < HEAD_DIM * (e + 1))
        vt_ref[LANES * h:LANES * (h + 1), :] = jnp.where(in_head_rows, vp, 1.0).astype(BF16)


def _prep(lf, lft, k, vt_in, b, s, tc):
    n, d_attn = k.shape
    n_heads = d_attn // HEAD_DIM
    nt = s // tc
    return pl.pallas_call(
        functools.partial(_prep_kernel, tc=tc, n_heads=n_heads),
        grid=(b, nt),
        in_specs=[pl.BlockSpec((tc, LANES), lambda i, j: (i * nt + j, 0)),
                  pl.BlockSpec((SUBLANES, tc), lambda i, j: (0, i * nt + j)),
                  pl.BlockSpec((tc, d_attn), lambda i, j: (i * nt + j, 0)),
                  pl.BlockSpec((1, d_attn, tc), lambda i, j: (i, 0, j))],
        out_specs=[pl.BlockSpec((SUBLANES, tc), lambda i, j: (0, i * nt + j)),
                   pl.BlockSpec((tc, n_heads * LANES), lambda i, j: (i * nt + j, 0)),
                   pl.BlockSpec((n_heads * LANES, tc), lambda i, j: (0, i * nt + j))],
        out_shape=[jax.ShapeDtypeStruct((SUBLANES, n), F32),
                   jax.ShapeDtypeStruct((n, n_heads * LANES), BF16),
                   jax.ShapeDtypeStruct((n_heads * LANES, n), BF16)],
        scratch_shapes=[pltpu.VMEM((1, LANES), F32), pltpu.VMEM((SUBLANES, 1), F32)],
        compiler_params=_params(("arbitrary", "arbitrary")),
        name="attn_prep",
    )(lf, lft, k, vt_in)


def _attn_kernel(q_ref, ka_ref, vt_ref, cr_ref, o_ref, qt_sc, m_sc, acc_sc, st_sc, *, tq, tk):
    hp = pl.program_id(1)
    qi = pl.program_id(2)
    lane = lax.broadcasted_iota(jnp.int32, (tq, LANES), 1)
    q2 = q_ref[...].astype(F32)
    for e in range(2):
        in_head, aug0 = _head_lanes(lane, e)
        is_aug = (lane >= aug0) & (lane < aug0 + N_AUG)
        qa = jnp.where(in_head, q2, jnp.where(is_aug, -1.0, 0.0))
        qt_sc[e] = qa.T.astype(BF16)
        m_sc[e] = jnp.full((1, tq), NEG_BIG, F32)
        acc_sc[e] = jnp.zeros((LANES, tq), F32)

    def scores(j, e):
        k0 = pl.multiple_of(j * tk, tk)
        ka = ka_ref[pl.ds(k0, tk), LANES * e:LANES * (e + 1)]
        return jnp.dot(ka, qt_sc[e], preferred_element_type=F32)

    def consume(st, j, e, diag):
        k0 = pl.multiple_of(j * tk, tk)
        if diag is not None:
            kr = lax.broadcasted_iota(jnp.int32, (tk, tq), 0) + diag * tk
            qc = lax.broadcasted_iota(jnp.int32, (tk, tq), 1)
            st = jnp.where(kr <= qc, st, -jnp.inf)
        cq = cr_ref[pl.ds(2 * hp + e, 1), :]
        m_prev = m_sc[e]
        m_new = jnp.maximum(m_prev, cq + jnp.max(st, axis=0, keepdims=True))
        p = jnp.exp(st - (m_new - cq))
        alpha = jnp.exp(m_prev - m_new)
        vt = vt_ref[LANES * e:LANES * (e + 1), pl.ds(k0, tk)]
        acc_sc[e] = alpha * acc_sc[e] + jnp.dot(vt, p.astype(BF16), preferred_element_type=F32)
        m_sc[e] = m_new

    def stage(j, slot, diag, has_next):
        for e in range(2):
            if has_next:
                st_sc[1 - slot, e] = scores(j + 1, e)
            consume(st_sc[slot, e], j, e, diag)

    nsub = tq // tk
    for e in range(2):
        st_sc[0, e] = scores(0, e)

    def body(jj, carry):
        for u in range(nsub):
            stage(jj * nsub + u, u % 2, None, True)
        return carry

    lax.fori_loop(0, qi, body, 0)
    for dd in range(nsub):
        stage(qi * nsub + dd, dd % 2, dd, dd + 1 < nsub)
    a0 = acc_sc[0]
    a1 = acc_sc[1]
    row = lax.broadcasted_iota(jnp.int32, (LANES, tq), 0)
    o_t = jnp.where(row < HEAD_DIM, a0 / a0[HEAD_DIM:HEAD_DIM + 1, :], a1 / a1[0:1, :])
    o_ref[...] = o_t.T.astype(BF16)


def _attention(q, ka, vt, cr, b, s, tq, tk):
    n, d_attn = q.shape
    assert (tq // tk) % 2 == 0, "the score double-buffer alternates per key block within a q tile"
    nq = s // tq
    npairs = d_attn // LANES
    return pl.pallas_call(
        functools.partial(_attn_kernel, tq=tq, tk=tk),
        grid=(b, npairs, nq),
        in_specs=[pl.BlockSpec((tq, LANES), lambda i, h, j: (i * nq + j, h)),
                  pl.BlockSpec((s, 2 * LANES), lambda i, h, j: (i, h)),
                  pl.BlockSpec((2 * LANES, s), lambda i, h, j: (h, i)),
                  pl.BlockSpec((SUBLANES, tq), lambda i, h, j: (0, i * nq + j))],
        out_specs=pl.BlockSpec((tq, LANES), lambda i, h, j: (i * nq + j, h)),
        out_shape=jax.ShapeDtypeStruct((n, d_attn), BF16),
        scratch_shapes=[pltpu.VMEM((2, LANES, tq), BF16), pltpu.VMEM((2, 1, tq), F32),
                        pltpu.VMEM((2, LANES, tq), F32), pltpu.VMEM((2, 2, tk, tq), F32)],
        compiler_params=_params(("arbitrary", "arbitrary", "arbitrary")),
        name="fox_attention",
    )(q, ka, vt, cr)


def _lane_roll_up(x, sh):
    return pltpu.roll(x, x.shape[-1] - sh, x.ndim - 1)


def _sample_attn_kernel(pt_ref, q_ref, kn_ref, vn_ref, lfn_ref, *rest, pages_per_step):
    del pt_ref
    pp = pages_per_step
    kt_refs = rest[:pp]
    vt_refs = rest[pp:2 * pp]
    lp_refs = rest[2 * pp:3 * pp]
    o_ref, m_sc, l_sc, acc_sc, carry_sc = rest[3 * pp:]
    g = pl.program_id(1)
    n_heads, hd, page = kt_refs[0].shape[1:]
    d_attn = n_heads * hd
    own = (lax.broadcasted_iota(jnp.int32, (n_heads, d_attn), 1) // hd
           == lax.broadcasted_iota(jnp.int32, (n_heads, d_attn), 0))
    qbd_f = jnp.where(own, jnp.broadcast_to(q_ref[0], (n_heads, d_attn)), 0.0)
    qbd = qbd_f.astype(BF16)

    @pl.when(g == 0)
    def _():
        kn = kn_ref[0].astype(BF16).astype(F32)
        m_sc[...] = jnp.sum(qbd_f * kn, axis=1, keepdims=True)
        l_sc[...] = jnp.ones_like(l_sc)
        acc_sc[...] = jnp.broadcast_to(vn_ref[0].astype(BF16).astype(F32), (n_heads, d_attn))
        carry_sc[...] = lfn_ref[0]

    lane = lax.broadcasted_iota(jnp.int32, (n_heads, page), 1)
    carry = carry_sc[...]
    scores, values = [], []
    for i in range(pp):
        kt = kt_refs[i][0].reshape(d_attn, page).astype(BF16)
        values.append(vt_refs[i][0].reshape(d_attn, page).astype(BF16))
        lp = lp_refs[i][0]
        suf = lp
        sh = 1
        while sh < page:
            suf = suf + jnp.where(lane + sh < page, _lane_roll_up(suf, sh), 0.0)
            sh *= 2
        scores.append(jnp.dot(qbd, kt, preferred_element_type=F32) + ((suf - lp) + carry))
        carry = carry + suf[:, 0:1]
    carry_sc[...] = carry
    smax = scores[0]
    for s in scores[1:]:
        smax = jnp.maximum(smax, s)
    m_prev = m_sc[...]
    m_new = jnp.maximum(m_prev, jnp.max(smax, axis=1, keepdims=True))
    alpha = jnp.exp(m_prev - m_new)
    psum = None
    pv = None
    for s, vt in zip(scores, values):
        p = jnp.exp(s - m_new)
        psum = p if psum is None else psum + p
        d = lax.dot_general(p.astype(BF16), vt, (((1,), (1,)), ((), ())), preferred_element_type=F32)
        pv = d if pv is None else pv + d
    l_sc[...] = alpha * l_sc[...] + jnp.sum(psum, axis=1, keepdims=True)
    acc_sc[...] = alpha * acc_sc[...] + pv
    m_sc[...] = m_new

    @pl.when(g == pl.num_programs(1) - 1)
    def _():
        o = jnp.where(own, acc_sc[...] / l_sc[...], 0.0)
        o_ref[0] = jnp.sum(o, axis=0, keepdims=True).astype(BF16)


def _sample_attention(q, kn, vn, lfn, ckt, cvt, clt, page_table, pages_per_step):
    db, d_attn = q.shape
    n_heads, hd, page = ckt.shape[1:]
    n_pages = page_table.shape[1]
    pp = pages_per_step
    steps = n_pages // pp
    pt = page_table.reshape(-1)

    def page_map(nd):
        def mk(i):
            return lambda b, g, pt_ref: (pt_ref[b * n_pages + (n_pages - 1 - (g * pp + i))],) + (0,) * nd
        return mk

    tok = lambda shape: pl.BlockSpec((1,) + shape, lambda b, g, pt_ref: (b,) + (0,) * len(shape))
    in_specs = [tok((1, d_attn)), tok((1, d_attn)), tok((1, d_attn)), tok((n_heads, 1))]
    in_specs += [pl.BlockSpec((1, n_heads, hd, page), page_map(3)(i)) for i in range(pp)]
    in_specs += [pl.BlockSpec((1, n_heads, hd, page), page_map(3)(i)) for i in range(pp)]
    in_specs += [pl.BlockSpec((1, n_heads, page), page_map(2)(i)) for i in range(pp)]
    out = pl.pallas_call(
        functools.partial(_sample_attn_kernel, pages_per_step=pp),
        grid_spec=pltpu.PrefetchScalarGridSpec(
            num_scalar_prefetch=1,
            grid=(db, steps),
            in_specs=in_specs,
            out_specs=tok((1, d_attn)),
            scratch_shapes=[pltpu.VMEM((n_heads, 1), F32), pltpu.VMEM((n_heads, 1), F32),
                            pltpu.VMEM((n_heads, d_attn), F32), pltpu.VMEM((n_heads, 1), F32)]),
        out_shape=jax.ShapeDtypeStruct((db, 1, d_attn), BF16),
        compiler_params=_params(("arbitrary", "arbitrary")),
        name="sample_attention",
    )(pt, q.reshape(db, 1, d_attn), kn.reshape(db, 1, d_attn), vn.reshape(db, 1, d_attn), lfn,
      *([ckt] * pp), *([cvt] * pp), *([clt] * pp))
    return out.reshape(db, d_attn)


def _lru_coeffs(pre_r, pre_i, xc, brg, big, neg_c_sp):
    r = _sigmoid(pre_r + brg)
    i = _sigmoid(pre_i + big)
    a = jnp.exp(r * neg_c_sp)
    mult = jnp.sqrt(1.0 - a * a)
    return a, mult, i * xc


def _gate_matmul(xcb, wg_ref, pre_ref):
    for j in range(wg_ref.shape[0]):
        lo = j * MXU_DIM
        pre_ref[:, 2 * lo:2 * lo + 2 * MXU_DIM] = jnp.dot(
            xcb[:, lo:lo + MXU_DIM], wg_ref[j], preferred_element_type=F32)


def _rnn_prompt_kernel(x_ref, g_ref, wxr_ref, wgr_ref, wg_ref, brg_ref, big_ref, ap_ref, cw_ref,
                       cb_ref, rnn_ref, conv_ref, h_ref,
                       xr_sc, xc_sc, pre_sc, gg_sc, hcar_sc, *, tt):
    t = pl.program_id(1)
    d = x_ref.shape[1]
    ngroups = tt // SUBLANES

    @pl.when(t == 0)
    def _():
        xr_sc[0:SUBLANES, :] = jnp.zeros((SUBLANES, d), F32)
        hcar_sc[...] = jnp.zeros_like(hcar_sc)

    xn = _rms(x_ref[...], g_ref[...]).astype(BF16)
    xr_sc[SUBLANES:, :] = jnp.dot(xn, wxr_ref[...], preferred_element_type=F32)
    gg_sc[...] = _gelu_tanh(jnp.dot(xn, wgr_ref[...], preferred_element_type=F32)).astype(BF16)

    sub = lax.broadcasted_iota(jnp.int32, (SUBLANES, d), 0)
    cw = cw_ref[...]
    cb = cb_ref[...]

    def conv_body(gi, carry):
        r0 = pl.multiple_of(gi * SUBLANES, SUBLANES)
        prev = xr_sc[pl.ds(r0, SUBLANES), :]
        cur = xr_sc[pl.ds(r0 + SUBLANES, SUBLANES), :]
        out = cb + cur * cw[CONV_WIDTH - 1:CONV_WIDTH, :]
        for sh in range(1, CONV_WIDTH):
            shifted = pltpu.roll(jnp.where(sub >= SUBLANES - sh, prev, cur), sh, 0)
            out = out + shifted * cw[CONV_WIDTH - 1 - sh:CONV_WIDTH - sh, :]
        xc_sc[pl.ds(r0, SUBLANES), :] = out
        return carry

    lax.fori_loop(0, ngroups, conv_body, 0)
    conv_ref[0] = xr_sc[tt:tt + SUBLANES, :]
    xr_sc[0:SUBLANES, :] = xr_sc[tt:tt + SUBLANES, :]

    _gate_matmul(xc_sc[...].astype(BF16), wg_ref, pre_sc)

    neg_c_sp = -LRU_C * _softplus(ap_ref[...])
    brg = brg_ref[...]
    big = big_ref[...]
    first_row = (sub == 0) & (t == 0)

    def lru_group(r0, is_first, hc):
        xc = xc_sc[pl.ds(r0, SUBLANES), :]
        pre = pre_sc[pl.ds(r0, SUBLANES), :]
        pre_r = jnp.concatenate([pre[:, 2 * j * MXU_DIM:(2 * j + 1) * MXU_DIM]
                                 for j in range(d // MXU_DIM)], axis=1)
        pre_i = jnp.concatenate([pre[:, (2 * j + 1) * MXU_DIM:(2 * j + 2) * MXU_DIM]
                                 for j in range(d // MXU_DIM)], axis=1)
        a, mult, ix = _lru_coeffs(pre_r, pre_i, xc, brg, big, neg_c_sp)
        if is_first is not None:
            mult = jnp.where(first_row & is_first, 1.0, mult)
        u = mult * ix
        for sh in (1, 2, 4):
            keep = sub >= sh
            a_s = jnp.where(keep, pltpu.roll(a, sh, 0), 1.0)
            u_s = jnp.where(keep, pltpu.roll(u, sh, 0), 0.0)
            u = u + a * u_s
            a = a * a_s
        h = u + a * hc
        return h, jnp.broadcast_to(h[SUBLANES - 1:SUBLANES, :], (SUBLANES, d))

    def lru_body(gi, hc):
        r0 = pl.multiple_of(gi * 2 * SUBLANES, 2 * SUBLANES)
        h0, hc = lru_group(r0, gi == 0, hc)
        h1, hc = lru_group(r0 + SUBLANES, None, hc)
        h = jnp.concatenate([h0, h1], axis=0)
        gg = gg_sc[pl.ds(r0, 2 * SUBLANES), :].astype(F32)
        rnn_ref[pl.ds(r0, 2 * SUBLANES), :] = (h * gg).astype(BF16)
        return hc

    hc = lax.fori_loop(0, ngroups // 2, lru_body, hcar_sc[...])
    hcar_sc[...] = hc
    h_ref[0] = hc


def _rnn_prompt(x, g, wxr, wgr, wg, brg, big, ap, cw, cb, b, s, tt):
    n, d = x.shape
    nt = s // tt
    vec = _const_spec((1, d))
    rnn, conv, h = pl.pallas_call(
        functools.partial(_rnn_prompt_kernel, tt=tt),
        grid=(b, nt),
        in_specs=[pl.BlockSpec((tt, d), lambda i, j: (i * nt + j, 0)), vec,
                  _const_spec(wxr.shape), _const_spec(wgr.shape), _const_spec(wg.shape),
                  vec, vec, vec, _const_spec(cw.shape), vec],
        out_specs=[pl.BlockSpec((tt, d), lambda i, j: (i * nt + j, 0)),
                   pl.BlockSpec((1, SUBLANES, d), lambda i, j: (i, 0, 0)),
                   pl.BlockSpec((1, SUBLANES, d), lambda i, j: (i, 0, 0))],
        out_shape=[jax.ShapeDtypeStruct((n, d), BF16),
                   jax.ShapeDtypeStruct((b, SUBLANES, d), F32),
                   jax.ShapeDtypeStruct((b, SUBLANES, d), F32)],
        scratch_shapes=[pltpu.VMEM((tt + SUBLANES, d), F32), pltpu.VMEM((tt, d), F32),
                        pltpu.VMEM((tt, 2 * d), F32), pltpu.VMEM((tt, d), BF16),
                        pltpu.VMEM((SUBLANES, d), F32)],
        compiler_params=_params(("arbitrary", "arbitrary")),
        name="rglru_prompt",
    )(x, g, wxr, wgr, wg, brg, big, ap, cw, cb)
    return rnn, conv[:, SUBLANES - (CONV_WIDTH - 1):, :], h[:, 0, :]


def _rnn_sample_kernel(x_ref, g_ref, wxr_ref, wgr_ref, wg_ref, brg_ref, big_ref, ap_ref, cw_ref,
                       cb_ref, sc_ref, h0_ref, rnn_ref, xr_ref, h_ref, pre_sc):
    d = x_ref.shape[1]
    xn = _rms(x_ref[...], g_ref[...]).astype(BF16)
    xr = jnp.dot(xn, wxr_ref[...], preferred_element_type=F32)
    gg = _gelu_tanh(jnp.dot(xn, wgr_ref[...], preferred_element_type=F32))
    cw = cw_ref[...]
    xc = cb_ref[...] + xr * cw[CONV_WIDTH - 1:CONV_WIDTH, :]
    for j in range(CONV_WIDTH - 1):
        xc = xc + sc_ref[j] * cw[j:j + 1, :]
    _gate_matmul(xc.astype(BF16), wg_ref, pre_sc)
    pre = pre_sc[...]
    pre_r = jnp.concatenate([pre[:, 2 * j * MXU_DIM:(2 * j + 1) * MXU_DIM]
                             for j in range(d // MXU_DIM)], axis=1)
    pre_i = jnp.concatenate([pre[:, (2 * j + 1) * MXU_DIM:(2 * j + 2) * MXU_DIM]
                             for j in range(d // MXU_DIM)], axis=1)
    a, mult, ix = _lru_coeffs(pre_r, pre_i, xc, brg_ref[...], big_ref[...],
                              -LRU_C * _softplus(ap_ref[...]))
    h = a * h0_ref[...] + mult * ix
    rnn_ref[...] = (h * gg).astype(BF16)
    xr_ref[...] = xr
    h_ref[...] = h


def _rnn_sample(x, g, wxr, wgr, wg, brg, big, ap, cw, cb, state_conv, h0):
    db, d = x.shape
    full = lambda shape: pl.BlockSpec(shape, lambda i: (0,) * len(shape))
    sc = jnp.swapaxes(state_conv, 0, 1)
    rnn, xr, h = pl.pallas_call(
        _rnn_sample_kernel,
        grid=(1,),
        in_specs=[full(a.shape) for a in (x, g, wxr, wgr, wg, brg, big, ap, cw, cb, sc, h0)],
        out_specs=[full((db, d))] * 3,
        out_shape=[jax.ShapeDtypeStruct((db, d), BF16), jax.ShapeDtypeStruct((db, d), F32),
                   jax.ShapeDtypeStruct((db, d), F32)],
        scratch_shapes=[pltpu.VMEM((db, 2 * d), F32)],
        compiler_params=_params(("arbitrary",)),
        name="rglru_sample",
    )(x, g, wxr, wgr, wg, brg, big, ap, cw, cb, sc, h0)
    conv_new = jnp.concatenate([state_conv[:, 1:, :], xr[:, None, :]], axis=1)
    return rnn, conv_new, h


def _post_kernel(x_ref, attn_ref, rnn_ref, ple_ref, g_pre_ref, wga_ref, wgrr_ref, wao_ref, wro_ref,
                 wout_ref, g_mixpost_ref, g_mlppre_ref, wff1_ref, wff2_ref, g_mlppost_ref,
                 g_plepre_ref, wpg_ref, wple_ref, g_plepost_ref, y_ref, *, ff_chunk):
    x = x_ref[...]
    xn = _rms(x, g_pre_ref[...]).astype(BF16)
    ga = _sigmoid(jnp.dot(xn, wga_ref[...], preferred_element_type=F32))
    m = ga * jnp.dot(attn_ref[...], wao_ref[...], preferred_element_type=F32)
    gr = _sigmoid(jnp.dot(xn, wgrr_ref[...], preferred_element_type=F32))
    m = m + gr * jnp.dot(rnn_ref[...], wro_ref[...], preferred_element_type=F32)
    x = x + _rms(jnp.dot(m.astype(BF16), wout_ref[...], preferred_element_type=F32),
                 g_mixpost_ref[...])
    hn = _rms(x, g_mlppre_ref[...]).astype(BF16)
    d_ff = wff1_ref.shape[1]
    f = None
    for c in range(d_ff // ff_chunk):
        lo = c * ff_chunk
        hid = jnp.dot(hn, wff1_ref[:, lo:lo + ff_chunk], preferred_element_type=F32)
        hid = jnp.square(jnp.maximum(hid, 0.0)).astype(BF16)
        part = jnp.dot(hid, wff2_ref[lo:lo + ff_chunk, :], preferred_element_type=F32)
        f = part if f is None else f + part
    x = x + _rms(f, g_mlppost_ref[...])
    gate = _sigmoid(jnp.dot(_rms(x, g_plepre_ref[...]).astype(BF16), wpg_ref[...],
                            preferred_element_type=F32))
    pe = jnp.dot(ple_ref[...].astype(BF16), wple_ref[...], preferred_element_type=F32)
    y_ref[...] = x + _rms(pe * gate, g_plepost_ref[...])


def _post(x, attn, rnn, ple, vecs, mats, tm):
    n, d = x.shape
    tok = lambda w: pl.BlockSpec((tm, w), lambda i: (i, 0))
    g_pre, g_mixpost, g_mlppre, g_mlppost, g_plepre, g_plepost = vecs
    wga, wgrr, wao, wro, wout, wff1, wff2, wpg, wple = mats
    vec = _const_spec((1, d))
    ops = (x, attn, rnn, ple, g_pre, wga, wgrr, wao, wro, wout, g_mixpost, g_mlppre, wff1, wff2,
           g_mlppost, g_plepre, wpg, wple, g_plepost)
    in_specs = [tok(d), tok(attn.shape[1]), tok(d), tok(ple.shape[1])]
    in_specs += [vec if a.shape == (1, d) else _const_spec(a.shape) for a in ops[4:]]
    return pl.pallas_call(
        functools.partial(_post_kernel, ff_chunk=min(1024, wff1.shape[1])),
        grid=(n // tm,),
        in_specs=in_specs,
        out_specs=tok(d),
        out_shape=jax.ShapeDtypeStruct((n, d), F32),
        compiler_params=_params(("arbitrary",)),
        name="merge_mlp_ple",
    )(*ops)


def _gate_blockdiag(w_rg, w_ig):
    nb = w_rg.shape[0]
    per = MXU_DIM // RNN_BLOCK
    eye = jnp.eye(per, dtype=F32)

    def bd(w):
        w = w.reshape(nb // per, per, RNN_BLOCK, RNN_BLOCK)
        t = jnp.einsum('jade,ab->jadbe', w, eye)
        return t.reshape(nb // per, MXU_DIM, MXU_DIM)

    return jnp.concatenate([bd(w_rg), bd(w_ig)], axis=2).astype(BF16)


def _pick(n, pref):
    t = min(n, pref)
    while n % t:
        t //= 2
    return t


def kernel(x_prompt, x_sample, p_prompt, p_sample, cache_k, cache_v, cache_logf, state_conv, state_h,
           page_table, g_mix_pre, w_in, b_f, w_rg, b_rg, w_ig, b_ig, a_param, conv_w, conv_b,
           w_attn_out, w_rnn_out, w_out, g_mix_post, g_mlp_pre, w_ff1, w_ff2, g_mlp_post,
           g_ple_pre, w_ple_gate, w_ple, g_ple_post):
    depth = w_in.shape[0]
    b, s, d = x_prompt.shape
    db = x_sample.shape[0]
    n_heads = b_f.shape[1]
    d_attn = n_heads * HEAD_DIM
    d_rnn = w_rg.shape[1] * RNN_BLOCK
    assert x_sample.shape[1] == 1 and d_rnn == d and n_heads == SUBLANES

    xp = x_prompt.reshape(b * s, d)
    xs = x_sample.reshape(db, d)
    outs = [[] for _ in range(10)]
    for l in range(depth):
        row = lambda a: a[l].reshape(1, -1)
        wl = w_in[l]
        o = 3 * d_attn
        wqkv = wl[:, :o].astype(BF16)
        wf = jnp.pad(wl[:, o:o + n_heads], ((0, 0), (0, LANES - n_heads))).astype(BF16)
        bfp = jnp.pad(b_f[l].reshape(1, -1), ((0, 0), (0, LANES - n_heads)))
        o += n_heads
        wxr = wl[:, o:o + d_rnn].astype(BF16)
        wgr = wl[:, o + d_rnn:o + 2 * d_rnn].astype(BF16)
        o += 2 * d_rnn
        wga = wl[:, o:o + d].astype(BF16)
        wgrr = wl[:, o + d:o + 2 * d].astype(BF16)
        wg = _gate_blockdiag(w_rg[l], w_ig[l])
        rnn_w = (row(g_mix_pre), wxr, wgr, wg, row(b_rg), row(b_ig), row(a_param), conv_w[l],
                 row(conv_b))
        vecs = (row(g_mix_pre), row(g_mix_post), row(g_mlp_pre), row(g_mlp_post), row(g_ple_pre),
                row(g_ple_post))
        mats = (wga, wgrr, w_attn_out[l].astype(BF16), w_rnn_out[l].astype(BF16),
                w_out[l].astype(BF16), w_ff1[l].astype(BF16), w_ff2[l].astype(BF16),
                w_ple_gate[l].astype(BF16), w_ple[l].astype(BF16))

        tm = _pick(s, 512)
        q, k, k_cm, v_cm, lf, lft = _inproj(xp, row(g_mix_pre), wqkv, wf, bfp, tm, (b, s))
        cr, ka, vt = _prep(lf, lft, k, v_cm, b, s, _pick(s, 512))
        tk = _pick(s, 512)
        attn = _attention(q, ka, vt, cr, b, s, _pick(s, 2 * tk), tk)
        rnn, conv_p, h_p = _rnn_prompt(xp, *rnn_w, b, s, _pick(s, 256))
        xp = _post(xp, attn, rnn, p_prompt[l].reshape(b * s, -1), vecs, mats, tm)
        seq_major = lambda a: jnp.transpose(a.reshape(b, n_heads, HEAD_DIM, s), (0, 3, 1, 2))
        for i, a in enumerate((seq_major(k_cm), seq_major(v_cm),
                               lf[:, :n_heads].reshape(b, s, n_heads), conv_p, h_p)):
            outs[i].append(a)

        qs, ks, vs, lfs, _ = _inproj(xs, row(g_mix_pre), wqkv, wf, bfp, db)
        lfs = lfs[:, :n_heads]
        attn_s = _sample_attention(
            qs.astype(F32), ks, vs, lfs.reshape(db, n_heads, 1),
            jnp.transpose(cache_k[l], (0, 2, 3, 1)), jnp.transpose(cache_v[l], (0, 2, 3, 1)),
            jnp.transpose(cache_logf[l], (0, 2, 1)), page_table, _pick(page_table.shape[1], 32))
        rnn_s, conv_s, h_s = _rnn_sample(xs, *rnn_w, state_conv[l], state_h[l].astype(F32))
        xs = _post(xs, attn_s, rnn_s, p_sample[l].reshape(db, -1), vecs, mats, db)
        for i, a in enumerate((ks.reshape(db, 1, n_heads, HEAD_DIM), vs.reshape(db, 1, n_heads, HEAD_DIM),
                               lfs.reshape(db, 1, n_heads), conv_s, h_s)):
            outs[5 + i].append(a)

    return (xp.reshape(b, s, d), xs.reshape(db, 1, d), *[jnp.stack(o) for o in outs])
```

```python
import functools
import math

import jax
import jax.numpy as jnp
from jax import lax
from jax.experimental import pallas as pl
from jax.experimental.pallas import tpu as pltpu

F32 = jnp.float32
BF16 = jnp.bfloat16
HIGHEST = lax.Precision.HIGHEST

RMS_EPS = 1e-6
LRU_C = 8.0
CONV_WIDTH = 4
RNN_BLOCK = 64
HEAD_DIM = 64
LANES = 128
SUBLANES = 8
MXU_DIM = 256
VMEM_LIMIT = 56 * 1024 * 1024
NEG_BIG = float(jnp.finfo(jnp.float32).min)
N_AUG = 3
VT_ROWS = 80
LOG2E = math.log2(math.e)


def _rms(x, g):
    ms = jnp.mean(x * x, axis=-1, keepdims=True)
    return x * lax.rsqrt(ms + RMS_EPS) * g


def _softplus(x):
    return jnp.maximum(x, 0.0) + jnp.log1p(jnp.exp(-jnp.abs(x)))


def _sigmoid(x):
    return 0.5 * jnp.tanh(0.5 * x) + 0.5


def _gelu_tanh(x):
    c = math.sqrt(2.0 / math.pi)
    return 0.5 * x * (1.0 + jnp.tanh(c * (x + 0.044715 * (x * x * x))))


def _const_spec(shape):
    nd = len(shape)
    return pl.BlockSpec(shape, lambda *_: (0,) * nd, pipeline_mode=pl.Buffered(1))


def _params(sem):
    return pltpu.CompilerParams(dimension_semantics=sem, vmem_limit_bytes=VMEM_LIMIT)


def _inproj_kernel(x_ref, g_ref, wqkv_ref, wf_ref, bf_ref, q_ref, k_ref, *rest, d_attn, q_scale,
                   channel_major):
    lf_ref, lft_ref = rest[-2:]
    xn = _rms(x_ref[...], g_ref[...]).astype(BF16)
    qkv = jnp.dot(xn, wqkv_ref[...], preferred_element_type=F32)
    q_ref[...] = (qkv[:, :d_attn] * q_scale).astype(BF16)
    k = qkv[:, d_attn:2 * d_attn]
    v = qkv[:, 2 * d_attn:]
    k_ref[...] = k
    if channel_major:
        kt_ref, vt_ref = rest[:2]
        kt_ref[0] = k.T
        vt_ref[0] = v.T
    else:
        rest[0][...] = v
    f = jnp.dot(xn, wf_ref[...], preferred_element_type=F32) + bf_ref[...]
    lf = -_softplus(-f)
    lf_ref[...] = lf
    lft_ref[...] = lf.T[:SUBLANES, :]


def _inproj(x, g, wqkv, wf, bfp, tm, q_scale, seqs=None):
    n, d = x.shape
    d_attn = wqkv.shape[1] // 3
    tok = lambda w: pl.BlockSpec((tm, w), lambda i: (i, 0))
    if seqs is None:
        kv_specs = [tok(d_attn)]
        kv_shapes = [jax.ShapeDtypeStruct((n, d_attn), F32)]
    else:
        b, s = seqs
        nt = s // tm
        kv_specs = [pl.BlockSpec((1, d_attn, tm), lambda i: (i // nt, 0, i % nt))] * 2
        kv_shapes = [jax.ShapeDtypeStruct((b, d_attn, s), F32)] * 2
    return pl.pallas_call(
        functools.partial(_inproj_kernel, d_attn=d_attn, q_scale=q_scale,
                          channel_major=seqs is not None),
        grid=(n // tm,),
        in_specs=[tok(d), _const_spec((1, d)), _const_spec(wqkv.shape), _const_spec(wf.shape),
                  _const_spec((1, LANES))],
        out_specs=[tok(d_attn), tok(d_attn), *kv_specs, tok(LANES),
                   pl.BlockSpec((SUBLANES, tm), lambda i: (0, i))],
        out_shape=[jax.ShapeDtypeStruct((n, d_attn), BF16),
                   jax.ShapeDtypeStruct((n, d_attn), F32),
                   *kv_shapes,
                   jax.ShapeDtypeStruct((n, LANES), F32),
                   jax.ShapeDtypeStruct((SUBLANES, n), F32)],
        compiler_params=_params(("arbitrary",)),
        name="inproj",
    )(x, g, wqkv, wf, bfp)


def _head_lanes(lane, e):
    in_head = (lane >= HEAD_DIM * e) & (lane < HEAD_DIM * (e + 1))
    aug0 = HEAD_DIM * (1 - e)
    return in_head, aug0


def _prep_kernel(lf_ref, lft_ref, k_ref, vt_in_ref, cr_ref, ka_ref, vt_ref, carry_c, carry_r,
                 *, tc, n_heads):
    @pl.when(pl.program_id(1) == 0)
    def _():
        carry_c[...] = jnp.zeros_like(carry_c)
        carry_r[...] = jnp.zeros_like(carry_r)

    row = lax.broadcasted_iota(jnp.int32, (tc, tc), 0)
    col = lax.broadcasted_iota(jnp.int32, (tc, tc), 1)
    lower = (col <= row).astype(F32)
    cc = jnp.dot(lower, lf_ref[...], precision=HIGHEST, preferred_element_type=F32) + carry_c[...]
    carry_c[...] = cc[tc - 1:tc, :]
    upper = (row <= col).astype(F32)
    cr = jnp.dot(lft_ref[...], upper, precision=HIGHEST, preferred_element_type=F32) + carry_r[...]
    cr_ref[...] = cr * LOG2E
    carry_r[...] = cr[:, tc - 1:tc]

    lane = lax.broadcasted_iota(jnp.int32, (tc, LANES), 1)
    for h in range(n_heads):
        e = h % 2
        in_head, aug0 = _head_lanes(lane, e)
        c = jnp.sum(jnp.where(lane == h, cc, 0.0), axis=1, keepdims=True) * LOG2E
        hi = c.astype(BF16).astype(F32)
        r1 = c - hi
        mid = r1.astype(BF16).astype(F32)
        lo = r1 - mid
        kp = k_ref[:, LANES * (h // 2):LANES * (h // 2 + 1)]
        aug = jnp.where(lane == aug0, hi,
                        jnp.where(lane == aug0 + 1, mid, jnp.where(lane == aug0 + 2, lo, 0.0)))
        ka_ref[:, LANES * h:LANES * (h + 1)] = jnp.where(in_head, kp, aug).astype(BF16)
        vt_ref[VT_ROWS * h:VT_ROWS * h + HEAD_DIM, :] = (
            vt_in_ref[0, HEAD_DIM * h:HEAD_DIM * (h + 1), :].astype(BF16))
        vt_ref[VT_ROWS * h + HEAD_DIM:VT_ROWS * (h + 1), :] = jnp.ones((VT_ROWS - HEAD_DIM, tc), BF16)


def _prep(lf, lft, k, vt_in, b, s, tc):
    n, d_attn = k.shape
    n_heads = d_attn // HEAD_DIM
    nt = s // tc
    return pl.pallas_call(
        functools.partial(_prep_kernel, tc=tc, n_heads=n_heads),
        grid=(b, nt),
        in_specs=[pl.BlockSpec((tc, LANES), lambda i, j: (i * nt + j, 0)),
                  pl.BlockSpec((SUBLANES, tc), lambda i, j: (0, i * nt + j)),
                  pl.BlockSpec((tc, d_attn), lambda i, j: (i * nt + j, 0)),
                  pl.BlockSpec((1, d_attn, tc), lambda i, j: (i, 0, j))],
        out_specs=[pl.BlockSpec((SUBLANES, tc), lambda i, j: (0, i * nt + j)),
                   pl.BlockSpec((tc, n_heads * LANES), lambda i, j: (i * nt + j, 0)),
                   pl.BlockSpec((n_heads * VT_ROWS, tc), lambda i, j: (0, i * nt + j))],
        out_shape=[jax.ShapeDtypeStruct((SUBLANES, n), F32),
                   jax.ShapeDtypeStruct((n, n_heads * LANES), BF16),
                   jax.ShapeDtypeStruct((n_heads * VT_ROWS, n), BF16)],
        scratch_shapes=[pltpu.VMEM((1, LANES), F32), pltpu.VMEM((SUBLANES, 1), F32)],
        compiler_params=_params(("arbitrary", "arbitrary")),
        name="attn_prep",
    )(lf, lft, k, vt_in)


def _attn_kernel(q_ref, ka_ref, vt_ref, cr_ref, o_ref, qt_sc, m_sc, acc_sc, st_sc, *, tq, tk):
    hp = pl.program_id(1)
    qi = pl.program_id(2)
    lane = lax.broadcasted_iota(jnp.int32, (tq, LANES), 1)
    q2 = q_ref[...].astype(F32)
    for e in range(2):
        in_head, aug0 = _head_lanes(lane, e)
        is_aug = (lane >= aug0) & (lane < aug0 + N_AUG)
        qa = jnp.where(in_head, q2, jnp.where(is_aug, -1.0, 0.0))
        qt_sc[e] = qa.T.astype(BF16)
        m_sc[e] = jnp.full((1, tq), NEG_BIG, F32)
        acc_sc[e] = jnp.zeros((VT_ROWS, tq), F32)

    def scores(j, e):
        k0 = pl.multiple_of(j * tk, tk)
        ka = ka_ref[pl.ds(k0, tk), LANES * e:LANES * (e + 1)]
        return jnp.dot(ka, qt_sc[e], preferred_element_type=F32)

    def consume(st, j, e, diag):
        k0 = pl.multiple_of(j * tk, tk)
        if diag is not None:
            kr = lax.broadcasted_iota(jnp.int32, (tk, tq), 0) + diag * tk
            qc = lax.broadcasted_iota(jnp.int32, (tk, tq), 1)
            st = jnp.where(kr <= qc, st, -jnp.inf)
        cq = cr_ref[pl.ds(2 * hp + e, 1), :]
        m_prev = m_sc[e]
        m_new = jnp.maximum(m_prev, cq + jnp.max(st, axis=0, keepdims=True))
        p = jnp.exp2(st - (m_new - cq))
        alpha = jnp.exp2(m_prev - m_new)
        vt = vt_ref[VT_ROWS * e:VT_ROWS * (e + 1), pl.ds(k0, tk)]
        acc_sc[e] = alpha * acc_sc[e] + jnp.dot(vt, p.astype(BF16), preferred_element_type=F32)
        m_sc[e] = m_new

    def stage(j, slot, diag, has_next):
        for e in range(2):
            if has_next:
                st_sc[1 - slot, e] = scores(j + 1, e)
            consume(st_sc[slot, e], j, e, diag)

    nsub = tq // tk
    for e in range(2):
        st_sc[0, e] = scores(0, e)

    def body(jj, carry):
        for u in range(nsub):
            stage(jj * nsub + u, u % 2, None, True)
        return carry

    lax.fori_loop(0, qi, body, 0)
    for dd in range(nsub):
        stage(qi * nsub + dd, dd % 2, dd, dd + 1 < nsub)
    o_t = jnp.concatenate([acc_sc[e][:HEAD_DIM, :] / acc_sc[e][HEAD_DIM:HEAD_DIM + 1, :]
                           for e in range(2)], axis=0)
    o_ref[...] = o_t.T.astype(BF16)


def _attention(q, ka, vt, cr, b, s, tq, tk):
    n, d_attn = q.shape
    assert (tq // tk) % 2 == 0, "the score double-buffer alternates per key block within a q tile"
    nq = s // tq
    npairs = d_attn // LANES
    return pl.pallas_call(
        functools.partial(_attn_kernel, tq=tq, tk=tk),
        grid=(b, npairs, nq),
        in_specs=[pl.BlockSpec((tq, LANES), lambda i, h, j: (i * nq + j, h)),
                  pl.BlockSpec((s, 2 * LANES), lambda i, h, j: (i, h)),
                  pl.BlockSpec((2 * VT_ROWS, s), lambda i, h, j: (h, i)),
                  pl.BlockSpec((SUBLANES, tq), lambda i, h, j: (0, i * nq + j))],
        out_specs=pl.BlockSpec((tq, LANES), lambda i, h, j: (i * nq + j, h)),
        out_shape=jax.ShapeDtypeStruct((n, d_attn), BF16),
        scratch_shapes=[pltpu.VMEM((2, LANES, tq), BF16), pltpu.VMEM((2, 1, tq), F32),
                        pltpu.VMEM((2, VT_ROWS, tq), F32), pltpu.VMEM((2, 2, tk, tq), F32)],
        compiler_params=_params(("arbitrary", "arbitrary", "arbitrary")),
        name="fox_attention",
    )(q, ka, vt, cr)


def _lane_roll_up(x, sh):
    return pltpu.roll(x, x.shape[-1] - sh, x.ndim - 1)


def _sample_attn_kernel(pt_ref, q_ref, kn_ref, vn_ref, lfn_ref, *rest, pages_per_step):
    del pt_ref
    pp = pages_per_step
    kt_refs = rest[:pp]
    vt_refs = rest[pp:2 * pp]
    lp_refs = rest[2 * pp:3 * pp]
    o_ref, m_sc, l_sc, acc_sc, carry_sc = rest[3 * pp:]
    g = pl.program_id(1)
    n_heads, hd, page = kt_refs[0].shape[1:]
    d_attn = n_heads * hd
    own = (lax.broadcasted_iota(jnp.int32, (n_heads, d_attn), 1) // hd
           == lax.broadcasted_iota(jnp.int32, (n_heads, d_attn), 0))
    qbd_f = jnp.where(own, jnp.broadcast_to(q_ref[0], (n_heads, d_attn)), 0.0)
    qbd = qbd_f.astype(BF16)

    @pl.when(g == 0)
    def _():
        kn = kn_ref[0].astype(BF16).astype(F32)
        m_sc[...] = jnp.sum(qbd_f * kn, axis=1, keepdims=True)
        l_sc[...] = jnp.ones_like(l_sc)
        acc_sc[...] = jnp.broadcast_to(vn_ref[0].astype(BF16).astype(F32), (n_heads, d_attn))
        carry_sc[...] = lfn_ref[0]

    lane = lax.broadcasted_iota(jnp.int32, (n_heads, page), 1)
    carry = carry_sc[...]
    scores, values = [], []
    for i in range(pp):
        kt = kt_refs[i][0].reshape(d_attn, page).astype(BF16)
        values.append(vt_refs[i][0].reshape(d_attn, page).astype(BF16))
        lp = lp_refs[i][0]
        suf = lp
        sh = 1
        while sh < page:
            suf = suf + jnp.where(lane + sh < page, _lane_roll_up(suf, sh), 0.0)
            sh *= 2
        scores.append(jnp.dot(qbd, kt, preferred_element_type=F32) + ((suf - lp) + carry))
        carry = carry + suf[:, 0:1]
    carry_sc[...] = carry
    smax = scores[0]
    for s in scores[1:]:
        smax = jnp.maximum(smax, s)
    m_prev = m_sc[...]
    m_new = jnp.maximum(m_prev, jnp.max(smax, axis=1, keepdims=True))
    alpha = jnp.exp(m_prev - m_new)
    psum = None
    pv = None
    for s, vt in zip(scores, values):
        p = jnp.exp(s - m_new)
        psum = p if psum is None else psum + p
        d = lax.dot_general(p.astype(BF16), vt, (((1,), (1,)), ((), ())), preferred_element_type=F32)
        pv = d if pv is None else pv + d
    l_sc[...] = alpha * l_sc[...] + jnp.sum(psum, axis=1, keepdims=True)
    acc_sc[...] = alpha * acc_sc[...] + pv
    m_sc[...] = m_new

    @pl.when(g == pl.num_programs(1) - 1)
    def _():
        o = jnp.where(own, acc_sc[...] / l_sc[...], 0.0)
        o_ref[0] = jnp.sum(o, axis=0, keepdims=True).astype(BF16)


def _sample_attention(q, kn, vn, lfn, ckt, cvt, clt, page_table, pages_per_step):
    db, d_attn = q.shape
    n_heads, hd, page = ckt.shape[1:]
    n_pages = page_table.shape[1]
    pp = pages_per_step
    steps = n_pages // pp
    pt = page_table.reshape(-1)

    def page_map(nd):
        def mk(i):
            return lambda b, g, pt_ref: (pt_ref[b * n_pages + (n_pages - 1 - (g * pp + i))],) + (0,) * nd
        return mk

    tok = lambda shape: pl.BlockSpec((1,) + shape, lambda b, g, pt_ref: (b,) + (0,) * len(shape))
    in_specs = [tok((1, d_attn)), tok((1, d_attn)), tok((1, d_attn)), tok((n_heads, 1))]
    in_specs += [pl.BlockSpec((1, n_heads, hd, page), page_map(3)(i)) for i in range(pp)]
    in_specs += [pl.BlockSpec((1, n_heads, hd, page), page_map(3)(i)) for i in range(pp)]
    in_specs += [pl.BlockSpec((1, n_heads, page), page_map(2)(i)) for i in range(pp)]
    out = pl.pallas_call(
        functools.partial(_sample_attn_kernel, pages_per_step=pp),
        grid_spec=pltpu.PrefetchScalarGridSpec(
            num_scalar_prefetch=1,
            grid=(db, steps),
            in_specs=in_specs,
            out_specs=tok((1, d_attn)),
            scratch_shapes=[pltpu.VMEM((n_heads, 1), F32), pltpu.VMEM((n_heads, 1), F32),
                            pltpu.VMEM((n_heads, d_attn), F32), pltpu.VMEM((n_heads, 1), F32)]),
        out_shape=jax.ShapeDtypeStruct((db, 1, d_attn), BF16),
        compiler_params=_params(("arbitrary", "arbitrary")),
        name="sample_attention",
    )(pt, q.reshape(db, 1, d_attn), kn.reshape(db, 1, d_attn), vn.reshape(db, 1, d_attn), lfn,
      *([ckt] * pp), *([cvt] * pp), *([clt] * pp))
    return out.reshape(db, d_attn)


def _lru_consts(brg, big, a_param):
    return 0.5 * brg, 0.5 * big, (-0.5 * LRU_C * LOG2E) * _softplus(a_param)


def _lru_coeffs(half_pre_r, half_pre_i, xc, half_brg, half_big, k):
    a = jnp.exp2(jnp.tanh(half_pre_r + half_brg) * k + k)
    xh = 0.5 * xc
    ix = jnp.tanh(half_pre_i + half_big) * xh + xh
    mult = jnp.sqrt(1.0 - a * a)
    return a, mult, ix


def _gate_matmul(xcb, wg_ref, pre_ref):
    for j in range(wg_ref.shape[0]):
        lo = j * MXU_DIM
        pre_ref[:, 2 * lo:2 * lo + 2 * MXU_DIM] = jnp.dot(
            xcb[:, lo:lo + MXU_DIM], wg_ref[j], preferred_element_type=F32)


def _rnn_prompt_kernel(x_ref, g_ref, wxr_ref, wgr_ref, wg_ref, brg_ref, big_ref, ap_ref, cw_ref,
                       cb_ref, rnn_ref, conv_ref, h_ref,
                       xr_sc, xc_sc, pre_sc, gg_sc, hcar_sc, vec_sc, *, tt):
    t = pl.program_id(1)
    d = x_ref.shape[1]
    ngroups = tt // SUBLANES

    @pl.when(t == 0)
    def _():
        xr_sc[0:SUBLANES, :] = jnp.zeros((SUBLANES, d), F32)
        hcar_sc[...] = jnp.zeros_like(hcar_sc)

    cw = cw_ref[...]
    vecs = [cw[j:j + 1, :] for j in range(CONV_WIDTH)] + [cb_ref[...]]
    vecs += list(_lru_consts(brg_ref[...], big_ref[...], ap_ref[...]))
    for i, v in enumerate(vecs):
        vec_sc[i] = jnp.broadcast_to(v, (SUBLANES, d))
    i_cb, i_brg, i_big, i_k = range(CONV_WIDTH, CONV_WIDTH + 4)

    xn = _rms(x_ref[...], g_ref[...]).astype(BF16)
    xr_sc[SUBLANES:, :] = jnp.dot(xn, wxr_ref[...], preferred_element_type=F32)
    gg_sc[...] = _gelu_tanh(jnp.dot(xn, wgr_ref[...], preferred_element_type=F32)).astype(BF16)

    sub = lax.broadcasted_iota(jnp.int32, (SUBLANES, d), 0)

    def conv_body(gi, carry):
        r0 = pl.multiple_of(gi * SUBLANES, SUBLANES)
        prev = xr_sc[pl.ds(r0, SUBLANES), :]
        cur = xr_sc[pl.ds(r0 + SUBLANES, SUBLANES), :]
        out = vec_sc[i_cb] + cur * vec_sc[CONV_WIDTH - 1]
        for sh in range(1, CONV_WIDTH):
            shifted = pltpu.roll(jnp.where(sub >= SUBLANES - sh, prev, cur), sh, 0)
            out = out + shifted * vec_sc[CONV_WIDTH - 1 - sh]
        xc_sc[pl.ds(r0, SUBLANES), :] = out
        return carry

    lax.fori_loop(0, ngroups, conv_body, 0)
    conv_ref[0] = xr_sc[tt:tt + SUBLANES, :]
    xr_sc[0:SUBLANES, :] = xr_sc[tt:tt + SUBLANES, :]

    _gate_matmul(xc_sc[...].astype(BF16), wg_ref, pre_sc)

    first_row = (sub == 0) & (t == 0)

    def lru_group(r0, is_first, hc):
        xc = xc_sc[pl.ds(r0, SUBLANES), :]
        pre = pre_sc[pl.ds(r0, SUBLANES), :]
        pre_r = jnp.concatenate([pre[:, 2 * j * MXU_DIM:(2 * j + 1) * MXU_DIM]
                                 for j in range(d // MXU_DIM)], axis=1)
        pre_i = jnp.concatenate([pre[:, (2 * j + 1) * MXU_DIM:(2 * j + 2) * MXU_DIM]
                                 for j in range(d // MXU_DIM)], axis=1)
        a, mult, ix = _lru_coeffs(pre_r, pre_i, xc, vec_sc[i_brg], vec_sc[i_big], vec_sc[i_k])
        if is_first is not None:
            mult = jnp.where(first_row & is_first, 1.0, mult)
        u = mult * ix
        for sh in (1, 2, 4):
            keep = sub >= sh
            a_s = jnp.where(keep, pltpu.roll(a, sh, 0), 1.0)
            u_s = jnp.where(keep, pltpu.roll(u, sh, 0), 0.0)
            u = u + a * u_s
            a = a * a_s
        h = u + a * hc
        return h, jnp.broadcast_to(h[SUBLANES - 1:SUBLANES, :], (SUBLANES, d))

    def lru_body(gi, hc):
        r0 = pl.multiple_of(gi * 2 * SUBLANES, 2 * SUBLANES)
        h0, hc = lru_group(r0, gi == 0, hc)
        h1, hc = lru_group(r0 + SUBLANES, None, hc)
        h = jnp.concatenate([h0, h1], axis=0)
        gg = gg_sc[pl.ds(r0, 2 * SUBLANES), :].astype(F32)
        rnn_ref[pl.ds(r0, 2 * SUBLANES), :] = (h * gg).astype(BF16)
        return hc

    hc = lax.fori_loop(0, ngroups // 2, lru_body, hcar_sc[...])
    hcar_sc[...] = hc
    h_ref[0] = hc


def _rnn_prompt(x, g, wxr, wgr, wg, brg, big, ap, cw, cb, b, s, tt):
    n, d = x.shape
    nt = s // tt
    vec = _const_spec((1, d))
    rnn, conv, h = pl.pallas_call(
        functools.partial(_rnn_prompt_kernel, tt=tt),
        grid=(b, nt),
        in_specs=[pl.BlockSpec((tt, d), lambda i, j: (i * nt + j, 0)), vec,
                  _const_spec(wxr.shape), _const_spec(wgr.shape), _const_spec(wg.shape),
                  vec, vec, vec, _const_spec(cw.shape), vec],
        out_specs=[pl.BlockSpec((tt, d), lambda i, j: (i * nt + j, 0)),
                   pl.BlockSpec((1, SUBLANES, d), lambda i, j: (i, 0, 0)),
                   pl.BlockSpec((1, SUBLANES, d), lambda i, j: (i, 0, 0))],
        out_shape=[jax.ShapeDtypeStruct((n, d), BF16),
                   jax.ShapeDtypeStruct((b, SUBLANES, d), F32),
                   jax.ShapeDtypeStruct((b, SUBLANES, d), F32)],
        scratch_shapes=[pltpu.VMEM((tt + SUBLANES, d), F32), pltpu.VMEM((tt, d), F32),
                        pltpu.VMEM((tt, 2 * d), F32), pltpu.VMEM((tt, d), BF16),
                        pltpu.VMEM((SUBLANES, d), F32),
                        pltpu.VMEM((CONV_WIDTH + 4, SUBLANES, d), F32)],
        compiler_params=_params(("arbitrary", "arbitrary")),
        name="rglru_prompt",
    )(x, g, wxr, wgr, wg, brg, big, ap, cw, cb)
    return rnn, conv[:, SUBLANES - (CONV_WIDTH - 1):, :], h[:, 0, :]


def _rnn_sample_kernel(x_ref, g_ref, wxr_ref, wgr_ref, wg_ref, brg_ref, big_ref, ap_ref, cw_ref,
                       cb_ref, sc_ref, h0_ref, rnn_ref, xr_ref, h_ref, pre_sc):
    d = x_ref.shape[1]
    xn = _rms(x_ref[...], g_ref[...]).astype(BF16)
    xr = jnp.dot(xn, wxr_ref[...], preferred_element_type=F32)
    gg = _gelu_tanh(jnp.dot(xn, wgr_ref[...], preferred_element_type=F32))
    cw = cw_ref[...]
    xc = cb_ref[...] + xr * cw[CONV_WIDTH - 1:CONV_WIDTH, :]
    for j in range(CONV_WIDTH - 1):
        xc = xc + sc_ref[j] * cw[j:j + 1, :]
    _gate_matmul(xc.astype(BF16), wg_ref, pre_sc)
    pre = pre_sc[...]
    pre_r = jnp.concatenate([pre[:, 2 * j * MXU_DIM:(2 * j + 1) * MXU_DIM]
                             for j in range(d // MXU_DIM)], axis=1)
    pre_i = jnp.concatenate([pre[:, (2 * j + 1) * MXU_DIM:(2 * j + 2) * MXU_DIM]
                             for j in range(d // MXU_DIM)], axis=1)
    a, mult, ix = _lru_coeffs(pre_r, pre_i, xc,
                              *_lru_consts(brg_ref[...], big_ref[...], ap_ref[...]))
    h = a * h0_ref[...] + mult * ix
    rnn_ref[...] = (h * gg).astype(BF16)
    xr_ref[...] = xr
    h_ref[...] = h


def _rnn_sample(x, g, wxr, wgr, wg, brg, big, ap, cw, cb, state_conv, h0):
    db, d = x.shape
    full = lambda shape: pl.BlockSpec(shape, lambda i: (0,) * len(shape))
    sc = jnp.swapaxes(state_conv, 0, 1)
    rnn, xr, h = pl.pallas_call(
        _rnn_sample_kernel,
        grid=(1,),
        in_specs=[full(a.shape) for a in (x, g, wxr, wgr, wg, brg, big, ap, cw, cb, sc, h0)],
        out_specs=[full((db, d))] * 3,
        out_shape=[jax.ShapeDtypeStruct((db, d), BF16), jax.ShapeDtypeStruct((db, d), F32),
                   jax.ShapeDtypeStruct((db, d), F32)],
        scratch_shapes=[pltpu.VMEM((db, 2 * d), F32)],
        compiler_params=_params(("arbitrary",)),
        name="rglru_sample",
    )(x, g, wxr, wgr, wg, brg, big, ap, cw, cb, sc, h0)
    conv_new = jnp.concatenate([state_conv[:, 1:, :], xr[:, None, :]], axis=1)
    return rnn, conv_new, h


def _post_kernel(x_ref, attn_ref, rnn_ref, ple_ref, g_pre_ref, wga_ref, wgrr_ref, wao_ref, wro_ref,
                 wout_ref, g_mixpost_ref, g_mlppre_ref, wff1_ref, wff2_ref, g_mlppost_ref,
                 g_plepre_ref, wpg_ref, wple_ref, g_plepost_ref, y_ref, *, ff_chunk):
    x = x_ref[...]
    xn = _rms(x, g_pre_ref[...]).astype(BF16)
    ga = _sigmoid(jnp.dot(xn, wga_ref[...], preferred_element_type=F32))
    m = ga * jnp.dot(attn_ref[...], wao_ref[...], preferred_element_type=F32)
    gr = _sigmoid(jnp.dot(xn, wgrr_ref[...], preferred_element_type=F32))
    m = m + gr * jnp.dot(rnn_ref[...], wro_ref[...], preferred_element_type=F32)
    x = x + _rms(jnp.dot(m.astype(BF16), wout_ref[...], preferred_element_type=F32),
                 g_mixpost_ref[...])
    hn = _rms(x, g_mlppre_ref[...]).astype(BF16)
    d_ff = wff1_ref.shape[1]
    f = None
    for c in range(d_ff // ff_chunk):
        lo = c * ff_chunk
        hid = jnp.dot(hn, wff1_ref[:, lo:lo + ff_chunk], preferred_element_type=F32)
        hid = jnp.square(jnp.maximum(hid, 0.0)).astype(BF16)
        part = jnp.dot(hid, wff2_ref[lo:lo + ff_chunk, :], preferred_element_type=F32)
        f = part if f is None else f + part
    x = x + _rms(f, g_mlppost_ref[...])
    gate = _sigmoid(jnp.dot(_rms(x, g_plepre_ref[...]).astype(BF16), wpg_ref[...],
                            preferred_element_type=F32))
    pe = jnp.dot(ple_ref[...].astype(BF16), wple_ref[...], preferred_element_type=F32)
    y_ref[...] = x + _rms(pe * gate, g_plepost_ref[...])


def _post(x, attn, rnn, ple, vecs, mats, tm):
    n, d = x.shape
    tok = lambda w: pl.BlockSpec((tm, w), lambda i: (i, 0))
    g_pre, g_mixpost, g_mlppre, g_mlppost, g_plepre, g_plepost = vecs
    wga, wgrr, wao, wro, wout, wff1, wff2, wpg, wple = mats
    vec = _const_spec((1, d))
    ops = (x, attn, rnn, ple, g_pre, wga, wgrr, wao, wro, wout, g_mixpost, g_mlppre, wff1, wff2,
           g_mlppost, g_plepre, wpg, wple, g_plepost)
    in_specs = [tok(d), tok(attn.shape[1]), tok(d), tok(ple.shape[1])]
    in_specs += [vec if a.shape == (1, d) else _const_spec(a.shape) for a in ops[4:]]
    return pl.pallas_call(
        functools.partial(_post_kernel, ff_chunk=min(1024, wff1.shape[1])),
        grid=(n // tm,),
        in_specs=in_specs,
        out_specs=tok(d),
        out_shape=jax.ShapeDtypeStruct((n, d), F32),
        compiler_params=_params(("arbitrary",)),
        name="merge_mlp_ple",
    )(*ops)


def _gate_blockdiag(w_rg, w_ig):
    nb = w_rg.shape[0]
    per = MXU_DIM // RNN_BLOCK
    eye = jnp.eye(per, dtype=F32)

    def bd(w):
        w = w.reshape(nb // per, per, RNN_BLOCK, RNN_BLOCK)
        t = jnp.einsum('jade,ab->jadbe', w, eye)
        return t.reshape(nb // per, MXU_DIM, MXU_DIM)

    return (0.5 * jnp.concatenate([bd(w_rg), bd(w_ig)], axis=2)).astype(BF16)


def _pick(n, pref):
    t = min(n, pref)
    while n % t:
        t //= 2
    return t


def kernel(x_prompt, x_sample, p_prompt, p_sample, cache_k, cache_v, cache_logf, state_conv, state_h,
           page_table, g_mix_pre, w_in, b_f, w_rg, b_rg, w_ig, b_ig, a_param, conv_w, conv_b,
           w_attn_out, w_rnn_out, w_out, g_mix_post, g_mlp_pre, w_ff1, w_ff2, g_mlp_post,
           g_ple_pre, w_ple_gate, w_ple, g_ple_post):
    depth = w_in.shape[0]
    b, s, d = x_prompt.shape
    db = x_sample.shape[0]
    n_heads = b_f.shape[1]
    d_attn = n_heads * HEAD_DIM
    d_rnn = w_rg.shape[1] * RNN_BLOCK
    assert x_sample.shape[1] == 1 and d_rnn == d and n_heads == SUBLANES

    xp = x_prompt.reshape(b * s, d)
    xs = x_sample.reshape(db, d)
    outs = [[] for _ in range(10)]
    for l in range(depth):
        row = lambda a: a[l].reshape(1, -1)
        wl = w_in[l]
        o = 3 * d_attn
        wqkv = wl[:, :o].astype(BF16)
        wf = jnp.pad(wl[:, o:o + n_heads], ((0, 0), (0, LANES - n_heads))).astype(BF16)
        bfp = jnp.pad(b_f[l].reshape(1, -1), ((0, 0), (0, LANES - n_heads)))
        o += n_heads
        wxr = wl[:, o:o + d_rnn].astype(BF16)
        wgr = wl[:, o + d_rnn:o + 2 * d_rnn].astype(BF16)
        o += 2 * d_rnn
        wga = wl[:, o:o + d].astype(BF16)
        wgrr = wl[:, o + d:o + 2 * d].astype(BF16)
        wg = _gate_blockdiag(w_rg[l], w_ig[l])
        rnn_w = (row(g_mix_pre), wxr, wgr, wg, row(b_rg), row(b_ig), row(a_param), conv_w[l],
                 row(conv_b))
        vecs = (row(g_mix_pre), row(g_mix_post), row(g_mlp_pre), row(g_mlp_post), row(g_ple_pre),
                row(g_ple_post))
        mats = (wga, wgrr, w_attn_out[l].astype(BF16), w_rnn_out[l].astype(BF16),
                w_out[l].astype(BF16), w_ff1[l].astype(BF16), w_ff2[l].astype(BF16),
                w_ple_gate[l].astype(BF16), w_ple[l].astype(BF16))

        tm = _pick(s, 512)
        inv_sqrt_d = 1.0 / math.sqrt(HEAD_DIM)
        q, k, k_cm, v_cm, lf, lft = _inproj(xp, row(g_mix_pre), wqkv, wf, bfp, tm,
                                            LOG2E * inv_sqrt_d, (b, s))
        cr, ka, vt = _prep(lf, lft, k, v_cm, b, s, _pick(s, 512))
        tk = _pick(s, 512)
        attn = _attention(q, ka, vt, cr, b, s, _pick(s, 2 * tk), tk)
        rnn, conv_p, h_p = _rnn_prompt(xp, *rnn_w, b, s, _pick(s, 256))
        xp = _post(xp, attn, rnn, p_prompt[l].reshape(b * s, -1), vecs, mats, tm)
        seq_major = lambda a: jnp.transpose(a.reshape(b, n_heads, HEAD_DIM, s), (0, 3, 1, 2))
        for i, a in enumerate((seq_major(k_cm), seq_major(v_cm),
                               lf[:, :n_heads].reshape(b, s, n_heads), conv_p, h_p)):
            outs[i].append(a)

        qs, ks, vs, lfs, _ = _inproj(xs, row(g_mix_pre), wqkv, wf, bfp, db, inv_sqrt_d)
        lfs = lfs[:, :n_heads]
        attn_s = _sample_attention(
            qs.astype(F32), ks, vs, lfs.reshape(db, n_heads, 1),
            jnp.transpose(cache_k[l], (0, 2, 3, 1)), jnp.transpose(cache_v[l], (0, 2, 3, 1)),
            jnp.transpose(cache_logf[l], (0, 2, 1)), page_table, _pick(page_table.shape[1], 32))
        rnn_s, conv_s, h_s = _rnn_sample(xs, *rnn_w, state_conv[l], state_h[l].astype(F32))
        xs = _post(xs, attn_s, rnn_s, p_sample[l].reshape(db, -1), vecs, mats, db)
        for i, a in enumerate((ks.reshape(db, 1, n_heads, HEAD_DIM), vs.reshape(db, 1, n_heads, HEAD_DIM),
                               lfs.reshape(db, 1, n_heads), conv_s, h_s)):
            outs[5 + i].append(a)

    return (xp.reshape(b, s, d), xs.reshape(db, 1, d), *[jnp.stack(o) for o in outs])
```

```python
import functools
import math

import jax
import jax.numpy as jnp
from jax import lax
from jax.experimental import pallas as pl
from jax.experimental.pallas import tpu as pltpu

F32 = jnp.float32
BF16 = jnp.bfloat16
HIGHEST = lax.Precision.HIGHEST

RMS_EPS = 1e-6
LRU_C = 8.0
CONV_WIDTH = 4
RNN_BLOCK = 64
HEAD_DIM = 64
LANES = 128
SUBLANES = 8
MXU_DIM = 256
VMEM_LIMIT = 56 * 1024 * 1024
NEG_BIG = float(jnp.finfo(jnp.float32).min)
N_AUG = 3
VT_ROWS = 80
LOG2E = math.log2(math.e)


def _rms(x, g):
    ms = jnp.mean(x * x, axis=-1, keepdims=True)
    return x * lax.rsqrt(ms + RMS_EPS) * g


def _softplus(x):
    return jnp.maximum(x, 0.0) + jnp.log1p(jnp.exp(-jnp.abs(x)))


def _sigmoid(x):
    return 0.5 * jnp.tanh(0.5 * x) + 0.5


def _gelu_tanh(x):
    c = math.sqrt(2.0 / math.pi)
    return 0.5 * x * (1.0 + jnp.tanh(c * (x + 0.044715 * (x * x * x))))


def _const_spec(shape):
    nd = len(shape)
    return pl.BlockSpec(shape, lambda *_: (0,) * nd, pipeline_mode=pl.Buffered(1))


def _params(sem):
    return pltpu.CompilerParams(dimension_semantics=sem, vmem_limit_bytes=VMEM_LIMIT)


def _inproj_kernel(x_ref, g_ref, wqkv_ref, wf_ref, bf_ref, q_ref, k_ref, *rest, d_attn, q_scale,
                   channel_major):
    lf_ref, lft_ref = rest[-2:]
    xn = _rms(x_ref[...], g_ref[...]).astype(BF16)
    qkv = jnp.dot(xn, wqkv_ref[...], preferred_element_type=F32)
    q_ref[...] = (qkv[:, :d_attn] * q_scale).astype(BF16)
    k = qkv[:, d_attn:2 * d_attn]
    v = qkv[:, 2 * d_attn:]
    k_ref[...] = k
    if channel_major:
        kt_ref, vt_ref = rest[:2]
        kt_ref[0] = k.T
        vt_ref[0] = v.T
    else:
        rest[0][...] = v
    f = jnp.dot(xn, wf_ref[...], preferred_element_type=F32) + bf_ref[...]
    lf = -_softplus(-f)
    lf_ref[...] = lf
    lft_ref[...] = lf.T[:SUBLANES, :]


def _inproj(x, g, wqkv, wf, bfp, tm, q_scale, seqs=None):
    n, d = x.shape
    d_attn = wqkv.shape[1] // 3
    tok = lambda w: pl.BlockSpec((tm, w), lambda i: (i, 0))
    if seqs is None:
        kv_specs = [tok(d_attn)]
        kv_shapes = [jax.ShapeDtypeStruct((n, d_attn), F32)]
    else:
        b, s = seqs
        nt = s // tm
        kv_specs = [pl.BlockSpec((1, d_attn, tm), lambda i: (i // nt, 0, i % nt))] * 2
        kv_shapes = [jax.ShapeDtypeStruct((b, d_attn, s), F32)] * 2
    return pl.pallas_call(
        functools.partial(_inproj_kernel, d_attn=d_attn, q_scale=q_scale,
                          channel_major=seqs is not None),
        grid=(n // tm,),
        in_specs=[tok(d), _const_spec((1, d)), _const_spec(wqkv.shape), _const_spec(wf.shape),
                  _const_spec((1, LANES))],
        out_specs=[tok(d_attn), tok(d_attn), *kv_specs, tok(LANES),
                   pl.BlockSpec((SUBLANES, tm), lambda i: (0, i))],
        out_shape=[jax.ShapeDtypeStruct((n, d_attn), BF16),
                   jax.ShapeDtypeStruct((n, d_attn), F32),
                   *kv_shapes,
                   jax.ShapeDtypeStruct((n, LANES), F32),
                   jax.ShapeDtypeStruct((SUBLANES, n), F32)],
        compiler_params=_params(("arbitrary",)),
        name="inproj",
    )(x, g, wqkv, wf, bfp)


def _head_lanes(lane, e):
    in_head = (lane >= HEAD_DIM * e) & (lane < HEAD_DIM * (e + 1))
    aug0 = HEAD_DIM * (1 - e)
    return in_head, aug0


def _prep_kernel(lf_ref, lft_ref, k_ref, vt_in_ref, cr_ref, ka_ref, vt_ref, carry_c, carry_r,
                 *, tc, n_heads):
    @pl.when(pl.program_id(1) == 0)
    def _():
        carry_c[...] = jnp.zeros_like(carry_c)
        carry_r[...] = jnp.zeros_like(carry_r)

    row = lax.broadcasted_iota(jnp.int32, (tc, tc), 0)
    col = lax.broadcasted_iota(jnp.int32, (tc, tc), 1)
    lower = (col <= row).astype(F32)
    cc = jnp.dot(lower, lf_ref[...], precision=HIGHEST, preferred_element_type=F32) + carry_c[...]
    carry_c[...] = cc[tc - 1:tc, :]
    upper = (row <= col).astype(F32)
    cr = jnp.dot(lft_ref[...], upper, precision=HIGHEST, preferred_element_type=F32) + carry_r[...]
    cr_ref[...] = cr * LOG2E
    carry_r[...] = cr[:, tc - 1:tc]

    lane = lax.broadcasted_iota(jnp.int32, (tc, LANES), 1)
    for h in range(n_heads):
        e = h % 2
        in_head, aug0 = _head_lanes(lane, e)
        c = jnp.sum(jnp.where(lane == h, cc, 0.0), axis=1, keepdims=True) * LOG2E
        hi = c.astype(BF16).astype(F32)
        r1 = c - hi
        mid = r1.astype(BF16).astype(F32)
        lo = r1 - mid
        kp = k_ref[:, LANES * (h // 2):LANES * (h // 2 + 1)]
        aug = jnp.where(lane == aug0, hi,
                        jnp.where(lane == aug0 + 1, mid, jnp.where(lane == aug0 + 2, lo, 0.0)))
        ka_ref[:, LANES * h:LANES * (h + 1)] = jnp.where(in_head, kp, aug).astype(BF16)
        vt_ref[VT_ROWS * h:VT_ROWS * h + HEAD_DIM, :] = (
            vt_in_ref[0, HEAD_DIM * h:HEAD_DIM * (h + 1), :].astype(BF16))
        vt_ref[VT_ROWS * h + HEAD_DIM:VT_ROWS * (h + 1), :] = jnp.ones((VT_ROWS - HEAD_DIM, tc), BF16)


def _prep(lf, lft, k, vt_in, b, s, tc):
    n, d_attn = k.shape
    n_heads = d_attn // HEAD_DIM
    nt = s // tc
    return pl.pallas_call(
        functools.partial(_prep_kernel, tc=tc, n_heads=n_heads),
        grid=(b, nt),
        in_specs=[pl.BlockSpec((tc, LANES), lambda i, j: (i * nt + j, 0)),
                  pl.BlockSpec((SUBLANES, tc), lambda i, j: (0, i * nt + j)),
                  pl.BlockSpec((tc, d_attn), lambda i, j: (i * nt + j, 0)),
                  pl.BlockSpec((1, d_attn, tc), lambda i, j: (i, 0, j))],
        out_specs=[pl.BlockSpec((SUBLANES, tc), lambda i, j: (0, i * nt + j)),
                   pl.BlockSpec((tc, n_heads * LANES), lambda i, j: (i * nt + j, 0)),
                   pl.BlockSpec((n_heads * VT_ROWS, tc), lambda i, j: (0, i * nt + j))],
        out_shape=[jax.ShapeDtypeStruct((SUBLANES, n), F32),
                   jax.ShapeDtypeStruct((n, n_heads * LANES), BF16),
                   jax.ShapeDtypeStruct((n_heads * VT_ROWS, n), BF16)],
        scratch_shapes=[pltpu.VMEM((1, LANES), F32), pltpu.VMEM((SUBLANES, 1), F32)],
        compiler_params=_params(("arbitrary", "arbitrary")),
        name="attn_prep",
    )(lf, lft, k, vt_in)


def _attn_kernel(q_ref, ka_ref, vt_ref, cr_ref, o_ref, qt_sc, m_sc, acc_sc, st_sc, *, tq, tk):
    hp = pl.program_id(1)
    qi = pl.program_id(2)
    lane = lax.broadcasted_iota(jnp.int32, (tq, LANES), 1)
    q2 = q_ref[...].astype(F32)
    for e in range(2):
        in_head, aug0 = _head_lanes(lane, e)
        is_aug = (lane >= aug0) & (lane < aug0 + N_AUG)
        qa = jnp.where(in_head, q2, jnp.where(is_aug, -1.0, 0.0))
        qt_sc[e] = qa.T.astype(BF16)
        m_sc[e] = jnp.full((1, tq), NEG_BIG, F32)
        acc_sc[e] = jnp.zeros((VT_ROWS, tq), F32)

    def scores(j, e, q_lo):
        k0 = pl.multiple_of(j * tk, tk)
        ka = ka_ref[pl.ds(k0, tk), LANES * e:LANES * (e + 1)]
        return jnp.dot(ka, qt_sc[e, :, q_lo:], preferred_element_type=F32)

    def consume(st, j, e, diag, q_lo):
        k0 = pl.multiple_of(j * tk, tk)
        if diag is not None:
            kr = lax.broadcasted_iota(jnp.int32, st.shape, 0) + diag * tk
            qc = lax.broadcasted_iota(jnp.int32, st.shape, 1) + q_lo
            st = jnp.where(kr <= qc, st, -jnp.inf)
        cq = cr_ref[pl.ds(2 * hp + e, 1), q_lo:]
        m_prev = m_sc[e, :, q_lo:]
        m_new = jnp.maximum(m_prev, cq + jnp.max(st, axis=0, keepdims=True))
        p = jnp.exp2(st - (m_new - cq))
        alpha = jnp.exp2(m_prev - m_new)
        vt = vt_ref[VT_ROWS * e:VT_ROWS * (e + 1), pl.ds(k0, tk)]
        acc_sc[e, :, q_lo:] = (alpha * acc_sc[e, :, q_lo:]
                               + jnp.dot(vt, p.astype(BF16), preferred_element_type=F32))
        m_sc[e, :, q_lo:] = m_new

    def stage(j, slot, diag, q_lo, next_q_lo):
        for e in range(2):
            if next_q_lo is not None:
                st_sc[1 - slot, e, :, next_q_lo:] = scores(j + 1, e, next_q_lo)
            consume(st_sc[slot, e, :, q_lo:], j, e, diag, q_lo)

    nsub = tq // tk
    for e in range(2):
        st_sc[0, e] = scores(0, e, 0)

    def body(jj, carry):
        for u in range(nsub):
            stage(jj * nsub + u, u % 2, None, 0, 0)
        return carry

    lax.fori_loop(0, qi, body, 0)
    for dd in range(nsub):
        stage(qi * nsub + dd, dd % 2, dd, dd * tk, (dd + 1) * tk if dd + 1 < nsub else None)
    o_t = jnp.concatenate([acc_sc[e][:HEAD_DIM, :] / acc_sc[e][HEAD_DIM:HEAD_DIM + 1, :]
                           for e in range(2)], axis=0)
    o_ref[...] = o_t.T.astype(BF16)


def _attention(q, ka, vt, cr, b, s, tq, tk):
    n, d_attn = q.shape
    assert (tq // tk) % 2 == 0, "the score double-buffer alternates per key block within a q tile"
    nq = s // tq
    npairs = d_attn // LANES
    return pl.pallas_call(
        functools.partial(_attn_kernel, tq=tq, tk=tk),
        grid=(b, npairs, nq),
        in_specs=[pl.BlockSpec((tq, LANES), lambda i, h, j: (i * nq + j, h)),
                  pl.BlockSpec((s, 2 * LANES), lambda i, h, j: (i, h)),
                  pl.BlockSpec((2 * VT_ROWS, s), lambda i, h, j: (h, i)),
                  pl.BlockSpec((SUBLANES, tq), lambda i, h, j: (0, i * nq + j))],
        out_specs=pl.BlockSpec((tq, LANES), lambda i, h, j: (i * nq + j, h)),
        out_shape=jax.ShapeDtypeStruct((n, d_attn), BF16),
        scratch_shapes=[pltpu.VMEM((2, LANES, tq), BF16), pltpu.VMEM((2, 1, tq), F32),
                        pltpu.VMEM((2, VT_ROWS, tq), F32), pltpu.VMEM((2, 2, tk, tq), F32)],
        compiler_params=_params(("arbitrary", "arbitrary", "arbitrary")),
        name="fox_attention",
    )(q, ka, vt, cr)


def _lane_roll_up(x, sh):
    return pltpu.roll(x, x.shape[-1] - sh, x.ndim - 1)


def _sample_attn_kernel(pt_ref, q_ref, kn_ref, vn_ref, lfn_ref, *rest, pages_per_step):
    del pt_ref
    pp = pages_per_step
    kt_refs = rest[:pp]
    vt_refs = rest[pp:2 * pp]
    lp_refs = rest[2 * pp:3 * pp]
    o_ref, m_sc, l_sc, acc_sc, carry_sc = rest[3 * pp:]
    g = pl.program_id(1)
    n_heads, hd, page = kt_refs[0].shape[1:]
    d_attn = n_heads * hd
    own = (lax.broadcasted_iota(jnp.int32, (n_heads, d_attn), 1) // hd
           == lax.broadcasted_iota(jnp.int32, (n_heads, d_attn), 0))
    qbd_f = jnp.where(own, jnp.broadcast_to(q_ref[0], (n_heads, d_attn)), 0.0)
    qbd = qbd_f.astype(BF16)

    @pl.when(g == 0)
    def _():
        kn = kn_ref[0].astype(BF16).astype(F32)
        m_sc[...] = jnp.sum(qbd_f * kn, axis=1, keepdims=True)
        l_sc[...] = jnp.ones_like(l_sc)
        acc_sc[...] = jnp.broadcast_to(vn_ref[0].astype(BF16).astype(F32), (n_heads, d_attn))
        carry_sc[...] = lfn_ref[0]

    lane = lax.broadcasted_iota(jnp.int32, (n_heads, page), 1)
    carry = carry_sc[...]
    scores, values = [], []
    for i in range(pp):
        kt = kt_refs[i][0].reshape(d_attn, page).astype(BF16)
        values.append(vt_refs[i][0].reshape(d_attn, page).astype(BF16))
        lp = lp_refs[i][0]
        suf = lp
        sh = 1
        while sh < page:
            suf = suf + jnp.where(lane + sh < page, _lane_roll_up(suf, sh), 0.0)
            sh *= 2
        scores.append(jnp.dot(qbd, kt, preferred_element_type=F32) + ((suf - lp) + carry))
        carry = carry + suf[:, 0:1]
    carry_sc[...] = carry
    smax = scores[0]
    for s in scores[1:]:
        smax = jnp.maximum(smax, s)
    m_prev = m_sc[...]
    m_new = jnp.maximum(m_prev, jnp.max(smax, axis=1, keepdims=True))
    alpha = jnp.exp(m_prev - m_new)
    psum = None
    pv = None
    for s, vt in zip(scores, values):
        p = jnp.exp(s - m_new)
        psum = p if psum is None else psum + p
        d = lax.dot_general(p.astype(BF16), vt, (((1,), (1,)), ((), ())), preferred_element_type=F32)
        pv = d if pv is None else pv + d
    l_sc[...] = alpha * l_sc[...] + jnp.sum(psum, axis=1, keepdims=True)
    acc_sc[...] = alpha * acc_sc[...] + pv
    m_sc[...] = m_new

    @pl.when(g == pl.num_programs(1) - 1)
    def _():
        o = jnp.where(own, acc_sc[...] / l_sc[...], 0.0)
        o_ref[0] = jnp.sum(o, axis=0, keepdims=True).astype(BF16)


def _sample_attention(q, kn, vn, lfn, ckt, cvt, clt, page_table, pages_per_step):
    db, d_attn = q.shape
    n_heads, hd, page = ckt.shape[1:]
    n_pages = page_table.shape[1]
    pp = pages_per_step
    steps = n_pages // pp
    pt = page_table.reshape(-1)

    def page_map(nd):
        def mk(i):
            return lambda b, g, pt_ref: (pt_ref[b * n_pages + (n_pages - 1 - (g * pp + i))],) + (0,) * nd
        return mk

    tok = lambda shape: pl.BlockSpec((1,) + shape, lambda b, g, pt_ref: (b,) + (0,) * len(shape))
    in_specs = [tok((1, d_attn)), tok((1, d_attn)), tok((1, d_attn)), tok((n_heads, 1))]
    in_specs += [pl.BlockSpec((1, n_heads, hd, page), page_map(3)(i)) for i in range(pp)]
    in_specs += [pl.BlockSpec((1, n_heads, hd, page), page_map(3)(i)) for i in range(pp)]
    in_specs += [pl.BlockSpec((1, n_heads, page), page_map(2)(i)) for i in range(pp)]
    out = pl.pallas_call(
        functools.partial(_sample_attn_kernel, pages_per_step=pp),
        grid_spec=pltpu.PrefetchScalarGridSpec(
            num_scalar_prefetch=1,
            grid=(db, steps),
            in_specs=in_specs,
            out_specs=tok((1, d_attn)),
            scratch_shapes=[pltpu.VMEM((n_heads, 1), F32), pltpu.VMEM((n_heads, 1), F32),
                            pltpu.VMEM((n_heads, d_attn), F32), pltpu.VMEM((n_heads, 1), F32)]),
        out_shape=jax.ShapeDtypeStruct((db, 1, d_attn), BF16),
        compiler_params=_params(("arbitrary", "arbitrary")),
        name="sample_attention",
    )(pt, q.reshape(db, 1, d_attn), kn.reshape(db, 1, d_attn), vn.reshape(db, 1, d_attn), lfn,
      *([ckt] * pp), *([cvt] * pp), *([clt] * pp))
    return out.reshape(db, d_attn)


def _lru_consts(brg, big, a_param):
    return 0.5 * brg, 0.5 * big, (-0.5 * LRU_C * LOG2E) * _softplus(a_param)


def _lru_coeffs(half_pre_r, half_pre_i, xc, half_brg, half_big, k):
    a = jnp.exp2(jnp.tanh(half_pre_r + half_brg) * k + k)
    xh = 0.5 * xc
    ix = jnp.tanh(half_pre_i + half_big) * xh + xh
    mult = jnp.sqrt(1.0 - a * a)
    return a, mult, ix


def _gate_matmul(xcb, wg_ref, pre_ref):
    for j in range(wg_ref.shape[0]):
        lo = j * MXU_DIM
        pre_ref[:, 2 * lo:2 * lo + 2 * MXU_DIM] = jnp.dot(
            xcb[:, lo:lo + MXU_DIM], wg_ref[j], preferred_element_type=F32)


def _rnn_prompt_kernel(x_ref, g_ref, wxr_ref, wgr_ref, wg_ref, brg_ref, big_ref, ap_ref, cw_ref,
                       cb_ref, rnn_ref, conv_ref, h_ref,
                       xr_sc, xc_sc, pre_sc, gg_sc, hcar_sc, vec_sc, *, tt):
    t = pl.program_id(1)
    d = x_ref.shape[1]
    ngroups = tt // SUBLANES

    @pl.when(t == 0)
    def _():
        xr_sc[0:SUBLANES, :] = jnp.zeros((SUBLANES, d), F32)
        hcar_sc[...] = jnp.zeros_like(hcar_sc)

    cw = cw_ref[...]
    vecs = [cw[j:j + 1, :] for j in range(CONV_WIDTH)] + [cb_ref[...]]
    vecs += list(_lru_consts(brg_ref[...], big_ref[...], ap_ref[...]))
    for i, v in enumerate(vecs):
        vec_sc[i] = jnp.broadcast_to(v, (SUBLANES, d))
    i_cb, i_brg, i_big, i_k = range(CONV_WIDTH, CONV_WIDTH + 4)

    xn = _rms(x_ref[...], g_ref[...]).astype(BF16)
    xr_sc[SUBLANES:, :] = jnp.dot(xn, wxr_ref[...], preferred_element_type=F32)
    gg_sc[...] = _gelu_tanh(jnp.dot(xn, wgr_ref[...], preferred_element_type=F32)).astype(BF16)

    sub = lax.broadcasted_iota(jnp.int32, (SUBLANES, d), 0)

    def conv_body(gi, carry):
        r0 = pl.multiple_of(gi * SUBLANES, SUBLANES)
        prev = xr_sc[pl.ds(r0, SUBLANES), :]
        cur = xr_sc[pl.ds(r0 + SUBLANES, SUBLANES), :]
        out = vec_sc[i_cb] + cur * vec_sc[CONV_WIDTH - 1]
        for sh in range(1, CONV_WIDTH):
            shifted = pltpu.roll(jnp.where(sub >= SUBLANES - sh, prev, cur), sh, 0)
            out = out + shifted * vec_sc[CONV_WIDTH - 1 - sh]
        xc_sc[pl.ds(r0, SUBLANES), :] = out
        return carry

    lax.fori_loop(0, ngroups, conv_body, 0)
    conv_ref[0] = xr_sc[tt:tt + SUBLANES, :]
    xr_sc[0:SUBLANES, :] = xr_sc[tt:tt + SUBLANES, :]

    _gate_matmul(xc_sc[...].astype(BF16), wg_ref, pre_sc)

    first_row = (sub == 0) & (t == 0)

    def lru_group(r0, is_first, hc):
        xc = xc_sc[pl.ds(r0, SUBLANES), :]
        pre = pre_sc[pl.ds(r0, SUBLANES), :]
        pre_r = jnp.concatenate([pre[:, 2 * j * MXU_DIM:(2 * j + 1) * MXU_DIM]
                                 for j in range(d // MXU_DIM)], axis=1)
        pre_i = jnp.concatenate([pre[:, (2 * j + 1) * MXU_DIM:(2 * j + 2) * MXU_DIM]
                                 for j in range(d // MXU_DIM)], axis=1)
        a, mult, ix = _lru_coeffs(pre_r, pre_i, xc, vec_sc[i_brg], vec_sc[i_big], vec_sc[i_k])
        if is_first is not None:
            mult = jnp.where(first_row & is_first, 1.0, mult)
        u = mult * ix
        for sh in (1, 2, 4):
            keep = sub >= sh
            a_s = jnp.where(keep, pltpu.roll(a, sh, 0), 1.0)
            u_s = jnp.where(keep, pltpu.roll(u, sh, 0), 0.0)
            u = u + a * u_s
            a = a * a_s
        h = u + a * hc
        return h, jnp.broadcast_to(h[SUBLANES - 1:SUBLANES, :], (SUBLANES, d))

    def lru_body(gi, hc):
        r0 = pl.multiple_of(gi * 2 * SUBLANES, 2 * SUBLANES)
        h0, hc = lru_group(r0, gi == 0, hc)
        h1, hc = lru_group(r0 + SUBLANES, None, hc)
        h = jnp.concatenate([h0, h1], axis=0)
        gg = gg_sc[pl.ds(r0, 2 * SUBLANES), :].astype(F32)
        rnn_ref[pl.ds(r0, 2 * SUBLANES), :] = (h * gg).astype(BF16)
        return hc

    hc = lax.fori_loop(0, ngroups // 2, lru_body, hcar_sc[...])
    hcar_sc[...] = hc
    h_ref[0] = hc


def _rnn_prompt(x, g, wxr, wgr, wg, brg, big, ap, cw, cb, b, s, tt):
    n, d = x.shape
    nt = s // tt
    vec = _const_spec((1, d))
    rnn, conv, h = pl.pallas_call(
        functools.partial(_rnn_prompt_kernel, tt=tt),
        grid=(b, nt),
        in_specs=[pl.BlockSpec((tt, d), lambda i, j: (i * nt + j, 0)), vec,
                  _const_spec(wxr.shape), _const_spec(wgr.shape), _const_spec(wg.shape),
                  vec, vec, vec, _const_spec(cw.shape), vec],
        out_specs=[pl.BlockSpec((tt, d), lambda i, j: (i * nt + j, 0)),
                   pl.BlockSpec((1, SUBLANES, d), lambda i, j: (i, 0, 0)),
                   pl.BlockSpec((1, SUBLANES, d), lambda i, j: (i, 0, 0))],
        out_shape=[jax.ShapeDtypeStruct((n, d), BF16),
                   jax.ShapeDtypeStruct((b, SUBLANES, d), F32),
                   jax.ShapeDtypeStruct((b, SUBLANES, d), F32)],
        scratch_shapes=[pltpu.VMEM((tt + SUBLANES, d), F32), pltpu.VMEM((tt, d), F32),
                        pltpu.VMEM((tt, 2 * d), F32), pltpu.VMEM((tt, d), BF16),
                        pltpu.VMEM((SUBLANES, d), F32),
                        pltpu.VMEM((CONV_WIDTH + 4, SUBLANES, d), F32)],
        compiler_params=_params(("arbitrary", "arbitrary")),
        name="rglru_prompt",
    )(x, g, wxr, wgr, wg, brg, big, ap, cw, cb)
    return rnn, conv[:, SUBLANES - (CONV_WIDTH - 1):, :], h[:, 0, :]


def _rnn_sample_kernel(x_ref, g_ref, wxr_ref, wgr_ref, wg_ref, brg_ref, big_ref, ap_ref, cw_ref,
                       cb_ref, sc_ref, h0_ref, rnn_ref, xr_ref, h_ref, pre_sc):
    d = x_ref.shape[1]
    xn = _rms(x_ref[...], g_ref[...]).astype(BF16)
    xr = jnp.dot(xn, wxr_ref[...], preferred_element_type=F32)
    gg = _gelu_tanh(jnp.dot(xn, wgr_ref[...], preferred_element_type=F32))
    cw = cw_ref[...]
    xc = cb_ref[...] + xr * cw[CONV_WIDTH - 1:CONV_WIDTH, :]
    for j in range(CONV_WIDTH - 1):
        xc = xc + sc_ref[j] * cw[j:j + 1, :]
    _gate_matmul(xc.astype(BF16), wg_ref, pre_sc)
    pre = pre_sc[...]
    pre_r = jnp.concatenate([pre[:, 2 * j * MXU_DIM:(2 * j + 1) * MXU_DIM]
                             for j in range(d // MXU_DIM)], axis=1)
    pre_i = jnp.concatenate([pre[:, (2 * j + 1) * MXU_DIM:(2 * j + 2) * MXU_DIM]
                             for j in range(d // MXU_DIM)], axis=1)
    a, mult, ix = _lru_coeffs(pre_r, pre_i, xc,
                              *_lru_consts(brg_ref[...], big_ref[...], ap_ref[...]))
    h = a * h0_ref[...] + mult * ix
    rnn_ref[...] = (h * gg).astype(BF16)
    xr_ref[...] = xr
    h_ref[...] = h


def _rnn_sample(x, g, wxr, wgr, wg, brg, big, ap, cw, cb, state_conv, h0):
    db, d = x.shape
    full = lambda shape: pl.BlockSpec(shape, lambda i: (0,) * len(shape))
    sc = jnp.swapaxes(state_conv, 0, 1)
    rnn, xr, h = pl.pallas_call(
        _rnn_sample_kernel,
        grid=(1,),
        in_specs=[full(a.shape) for a in (x, g, wxr, wgr, wg, brg, big, ap, cw, cb, sc, h0)],
        out_specs=[full((db, d))] * 3,
        out_shape=[jax.ShapeDtypeStruct((db, d), BF16), jax.ShapeDtypeStruct((db, d), F32),
                   jax.ShapeDtypeStruct((db, d), F32)],
        scratch_shapes=[pltpu.VMEM((db, 2 * d), F32)],
        compiler_params=_params(("arbitrary",)),
        name="rglru_sample",
    )(x, g, wxr, wgr, wg, brg, big, ap, cw, cb, sc, h0)
    conv_new = jnp.concatenate([state_conv[:, 1:, :], xr[:, None, :]], axis=1)
    return rnn, conv_new, h


def _post_kernel(x_ref, attn_ref, rnn_ref, ple_ref, g_pre_ref, wga_ref, wgrr_ref, wao_ref, wro_ref,
                 wout_ref, g_mixpost_ref, g_mlppre_ref, wff1_ref, wff2_ref, g_mlppost_ref,
                 g_plepre_ref, wpg_ref, wple_ref, g_plepost_ref, y_ref, *, ff_chunk):
    x = x_ref[...]
    xn = _rms(x, g_pre_ref[...]).astype(BF16)
    ga = _sigmoid(jnp.dot(xn, wga_ref[...], preferred_element_type=F32))
    m = ga * jnp.dot(attn_ref[...], wao_ref[...], preferred_element_type=F32)
    gr = _sigmoid(jnp.dot(xn, wgrr_ref[...], preferred_element_type=F32))
    m = m + gr * jnp.dot(rnn_ref[...], wro_ref[...], preferred_element_type=F32)
    x = x + _rms(jnp.dot(m.astype(BF16), wout_ref[...], preferred_element_type=F32),
                 g_mixpost_ref[...])
    hn = _rms(x, g_mlppre_ref[...]).astype(BF16)
    d_ff = wff1_ref.shape[1]
    f = None
    for c in range(d_ff // ff_chunk):
        lo = c * ff_chunk
        hid = jnp.dot(hn, wff1_ref[:, lo:lo + ff_chunk], preferred_element_type=F32)
        hid = jnp.square(jnp.maximum(hid, 0.0)).astype(BF16)
        part = jnp.dot(hid, wff2_ref[lo:lo + ff_chunk, :], preferred_element_type=F32)
        f = part if f is None else f + part
    x = x + _rms(f, g_mlppost_ref[...])
    gate = _sigmoid(jnp.dot(_rms(x, g_plepre_ref[...]).astype(BF16), wpg_ref[...],
                            preferred_element_type=F32))
    pe = jnp.dot(ple_ref[...].astype(BF16), wple_ref[...], preferred_element_type=F32)
    y_ref[...] = x + _rms(pe * gate, g_plepost_ref[...])


def _post(x, attn, rnn, ple, vecs, mats, tm):
    n, d = x.shape
    tok = lambda w: pl.BlockSpec((tm, w), lambda i: (i, 0))
    g_pre, g_mixpost, g_mlppre, g_mlppost, g_plepre, g_plepost = vecs
    wga, wgrr, wao, wro, wout, wff1, wff2, wpg, wple = mats
    vec = _const_spec((1, d))
    ops = (x, attn, rnn, ple, g_pre, wga, wgrr, wao, wro, wout, g_mixpost, g_mlppre, wff1, wff2,
           g_mlppost, g_plepre, wpg, wple, g_plepost)
    in_specs = [tok(d), tok(attn.shape[1]), tok(d), tok(ple.shape[1])]
    in_specs += [vec if a.shape == (1, d) else _const_spec(a.shape) for a in ops[4:]]
    return pl.pallas_call(
        functools.partial(_post_kernel, ff_chunk=min(1024, wff1.shape[1])),
        grid=(n // tm,),
        in_specs=in_specs,
        out_specs=tok(d),
        out_shape=jax.ShapeDtypeStruct((n, d), F32),
        compiler_params=_params(("arbitrary",)),
        name="merge_mlp_ple",
    )(*ops)


def _gate_blockdiag(w_rg, w_ig):
    nb = w_rg.shape[0]
    per = MXU_DIM // RNN_BLOCK
    eye = jnp.eye(per, dtype=F32)

    def bd(w):
        w = w.reshape(nb // per, per, RNN_BLOCK, RNN_BLOCK)
        t = jnp.einsum('jade,ab->jadbe', w, eye)
        return t.reshape(nb // per, MXU_DIM, MXU_DIM)

    return (0.5 * jnp.concatenate([bd(w_rg), bd(w_ig)], axis=2)).astype(BF16)


def _pick(n, pref):
    t = min(n, pref)
    while n % t:
        t //= 2
    return t


def kernel(x_prompt, x_sample, p_prompt, p_sample, cache_k, cache_v, cache_logf, state_conv, state_h,
           page_table, g_mix_pre, w_in, b_f, w_rg, b_rg, w_ig, b_ig, a_param, conv_w, conv_b,
           w_attn_out, w_rnn_out, w_out, g_mix_post, g_mlp_pre, w_ff1, w_ff2, g_mlp_post,
           g_ple_pre, w_ple_gate, w_ple, g_ple_post):
    depth = w_in.shape[0]
    b, s, d = x_prompt.shape
    db = x_sample.shape[0]
    n_heads = b_f.shape[1]
    d_attn = n_heads * HEAD_DIM
    d_rnn = w_rg.shape[1] * RNN_BLOCK
    assert x_sample.shape[1] == 1 and d_rnn == d and n_heads == SUBLANES

    xp = x_prompt.reshape(b * s, d)
    xs = x_sample.reshape(db, d)
    outs = [[] for _ in range(10)]
    for l in range(depth):
        row = lambda a: a[l].reshape(1, -1)
        wl = w_in[l]
        o = 3 * d_attn
        wqkv = wl[:, :o].astype(BF16)
        wf = jnp.pad(wl[:, o:o + n_heads], ((0, 0), (0, LANES - n_heads))).astype(BF16)
        bfp = jnp.pad(b_f[l].reshape(1, -1), ((0, 0), (0, LANES - n_heads)))
        o += n_heads
        wxr = wl[:, o:o + d_rnn].astype(BF16)
        wgr = wl[:, o + d_rnn:o + 2 * d_rnn].astype(BF16)
        o += 2 * d_rnn
        wga = wl[:, o:o + d].astype(BF16)
        wgrr = wl[:, o + d:o + 2 * d].astype(BF16)
        wg = _gate_blockdiag(w_rg[l], w_ig[l])
        rnn_w = (row(g_mix_pre), wxr, wgr, wg, row(b_rg), row(b_ig), row(a_param), conv_w[l],
                 row(conv_b))
        vecs = (row(g_mix_pre), row(g_mix_post), row(g_mlp_pre), row(g_mlp_post), row(g_ple_pre),
                row(g_ple_post))
        mats = (wga, wgrr, w_attn_out[l].astype(BF16), w_rnn_out[l].astype(BF16),
                w_out[l].astype(BF16), w_ff1[l].astype(BF16), w_ff2[l].astype(BF16),
                w_ple_gate[l].astype(BF16), w_ple[l].astype(BF16))

        tm = _pick(s, 512)
        inv_sqrt_d = 1.0 / math.sqrt(HEAD_DIM)
        q, k, k_cm, v_cm, lf, lft = _inproj(xp, row(g_mix_pre), wqkv, wf, bfp, tm,
                                            LOG2E * inv_sqrt_d, (b, s))
        cr, ka, vt = _prep(lf, lft, k, v_cm, b, s, _pick(s, 512))
        tk = _pick(s, 512)
        attn = _attention(q, ka, vt, cr, b, s, _pick(s, 2 * tk), tk)
        rnn, conv_p, h_p = _rnn_prompt(xp, *rnn_w, b, s, _pick(s, 512))
        xp = _post(xp, attn, rnn, p_prompt[l].reshape(b * s, -1), vecs, mats, tm)
        seq_major = lambda a: jnp.transpose(a.reshape(b, n_heads, HEAD_DIM, s), (0, 3, 1, 2))
        for i, a in enumerate((seq_major(k_cm), seq_major(v_cm),
                               lf[:, :n_heads].reshape(b, s, n_heads), conv_p, h_p)):
            outs[i].append(a)

        qs, ks, vs, lfs, _ = _inproj(xs, row(g_mix_pre), wqkv, wf, bfp, db, inv_sqrt_d)
        lfs = lfs[:, :n_heads]
        attn_s = _sample_attention(
            qs.astype(F32), ks, vs, lfs.reshape(db, n_heads, 1),
            jnp.transpose(cache_k[l], (0, 2, 3, 1)), jnp.transpose(cache_v[l], (0, 2, 3, 1)),
            jnp.transpose(cache_logf[l], (0, 2, 1)), page_table, _pick(page_table.shape[1], 32))
        rnn_s, conv_s, h_s = _rnn_sample(xs, *rnn_w, state_conv[l], state_h[l].astype(F32))
        xs = _post(xs, attn_s, rnn_s, p_sample[l].reshape(db, -1), vecs, mats, db)
        for i, a in enumerate((ks.reshape(db, 1, n_heads, HEAD_DIM), vs.reshape(db, 1, n_heads, HEAD_DIM),
                               lfs.reshape(db, 1, n_heads), conv_s, h_s)):
            outs[5 + i].append(a)

    return (xp.reshape(b, s, d), xs.reshape(db, 1, d), *[jnp.stack(o) for o in outs])
```

```python
import functools
import math

import jax
import jax.numpy as jnp
from jax import lax
from jax.experimental import pallas as pl
from jax.experimental.pallas import tpu as pltpu

F32 = jnp.float32
BF16 = jnp.bfloat16
HIGHEST = lax.Precision.HIGHEST

RMS_EPS = 1e-6
LRU_C = 8.0
CONV_WIDTH = 4
RNN_BLOCK = 64
HEAD_DIM = 64
LANES = 128
SUBLANES = 8
MXU_DIM = 256
VMEM_LIMIT = 56 * 1024 * 1024
NEG_BIG = float(jnp.finfo(jnp.float32).min)
N_AUG = 3
VT_ROWS = 80
LOG2E = math.log2(math.e)


def _rms(x, g):
    ms = jnp.mean(x * x, axis=-1, keepdims=True)
    return x * lax.rsqrt(ms + RMS_EPS) * g


def _softplus(x):
    return jnp.maximum(x, 0.0) + jnp.log1p(jnp.exp(-jnp.abs(x)))


def _sigmoid(x):
    return 0.5 * jnp.tanh(0.5 * x) + 0.5


def _gelu_tanh(x):
    c = math.sqrt(2.0 / math.pi)
    return 0.5 * x * (1.0 + jnp.tanh(c * (x + 0.044715 * (x * x * x))))


def _const_spec(shape):
    nd = len(shape)
    return pl.BlockSpec(shape, lambda *_: (0,) * nd, pipeline_mode=pl.Buffered(1))


def _params(sem):
    return pltpu.CompilerParams(dimension_semantics=sem, vmem_limit_bytes=VMEM_LIMIT)


def _inproj_kernel(x_ref, g_ref, wqkv_ref, wf_ref, bf_ref, q_ref, k_ref, *rest, d_attn, q_scale,
                   channel_major):
    lf_ref, lft_ref = rest[-2:]
    xn = _rms(x_ref[...], g_ref[...]).astype(BF16)
    qkv = jnp.dot(xn, wqkv_ref[...], preferred_element_type=F32)
    q_ref[...] = (qkv[:, :d_attn] * q_scale).astype(BF16)
    k = qkv[:, d_attn:2 * d_attn]
    v = qkv[:, 2 * d_attn:]
    k_ref[...] = k
    if channel_major:
        kt_ref, vt_ref = rest[:2]
        kt_ref[0] = k.T
        vt_ref[0] = v.T
    else:
        rest[0][...] = v
    f = jnp.dot(xn, wf_ref[...], preferred_element_type=F32) + bf_ref[...]
    lf = -_softplus(-f)
    lf_ref[...] = lf
    lft_ref[...] = lf.T[:SUBLANES, :]


def _inproj(x, g, wqkv, wf, bfp, tm, q_scale, seqs=None):
    n, d = x.shape
    d_attn = wqkv.shape[1] // 3
    tok = lambda w: pl.BlockSpec((tm, w), lambda i: (i, 0))
    if seqs is None:
        kv_specs = [tok(d_attn)]
        kv_shapes = [jax.ShapeDtypeStruct((n, d_attn), F32)]
    else:
        b, s = seqs
        nt = s // tm
        kv_specs = [pl.BlockSpec((1, d_attn, tm), lambda i: (i // nt, 0, i % nt))] * 2
        kv_shapes = [jax.ShapeDtypeStruct((b, d_attn, s), F32)] * 2
    return pl.pallas_call(
        functools.partial(_inproj_kernel, d_attn=d_attn, q_scale=q_scale,
                          channel_major=seqs is not None),
        grid=(n // tm,),
        in_specs=[tok(d), _const_spec((1, d)), _const_spec(wqkv.shape), _const_spec(wf.shape),
                  _const_spec((1, LANES))],
        out_specs=[tok(d_attn), tok(d_attn), *kv_specs, tok(LANES),
                   pl.BlockSpec((SUBLANES, tm), lambda i: (0, i))],
        out_shape=[jax.ShapeDtypeStruct((n, d_attn), BF16),
                   jax.ShapeDtypeStruct((n, d_attn), F32),
                   *kv_shapes,
                   jax.ShapeDtypeStruct((n, LANES), F32),
                   jax.ShapeDtypeStruct((SUBLANES, n), F32)],
        compiler_params=_params(("arbitrary",)),
        name="inproj",
    )(x, g, wqkv, wf, bfp)


def _head_lanes(lane, e):
    in_head = (lane >= HEAD_DIM * e) & (lane < HEAD_DIM * (e + 1))
    aug0 = HEAD_DIM * (1 - e)
    return in_head, aug0


def _prep_kernel(lf_ref, lft_ref, k_ref, vt_in_ref, cr_ref, ka_ref, vt_ref, carry_c, carry_r,
                 *, tc, n_heads):
    @pl.when(pl.program_id(1) == 0)
    def _():
        carry_c[...] = jnp.zeros_like(carry_c)
        carry_r[...] = jnp.zeros_like(carry_r)

    row = lax.broadcasted_iota(jnp.int32, (tc, tc), 0)
    col = lax.broadcasted_iota(jnp.int32, (tc, tc), 1)
    lower = (col <= row).astype(F32)
    cc = jnp.dot(lower, lf_ref[...], precision=HIGHEST, preferred_element_type=F32) + carry_c[...]
    carry_c[...] = cc[tc - 1:tc, :]
    upper = (row <= col).astype(F32)
    cr = jnp.dot(lft_ref[...], upper, precision=HIGHEST, preferred_element_type=F32) + carry_r[...]
    cr_ref[...] = cr * LOG2E
    carry_r[...] = cr[:, tc - 1:tc]

    lane = lax.broadcasted_iota(jnp.int32, (tc, LANES), 1)
    for h in range(n_heads):
        e = h % 2
        in_head, aug0 = _head_lanes(lane, e)
        c = jnp.sum(jnp.where(lane == h, cc, 0.0), axis=1, keepdims=True) * LOG2E
        hi = c.astype(BF16).astype(F32)
        r1 = c - hi
        mid = r1.astype(BF16).astype(F32)
        lo = r1 - mid
        kp = k_ref[:, LANES * (h // 2):LANES * (h // 2 + 1)]
        aug = jnp.where(lane == aug0, hi,
                        jnp.where(lane == aug0 + 1, mid, jnp.where(lane == aug0 + 2, lo, 0.0)))
        ka_ref[:, LANES * h:LANES * (h + 1)] = jnp.where(in_head, kp, aug).astype(BF16)
        vt_ref[VT_ROWS * h:VT_ROWS * h + HEAD_DIM, :] = (
            vt_in_ref[0, HEAD_DIM * h:HEAD_DIM * (h + 1), :].astype(BF16))
        vt_ref[VT_ROWS * h + HEAD_DIM:VT_ROWS * (h + 1), :] = jnp.ones((VT_ROWS - HEAD_DIM, tc), BF16)


def _prep(lf, lft, k, vt_in, b, s, tc):
    n, d_attn = k.shape
    n_heads = d_attn // HEAD_DIM
    nt = s // tc
    return pl.pallas_call(
        functools.partial(_prep_kernel, tc=tc, n_heads=n_heads),
        grid=(b, nt),
        in_specs=[pl.BlockSpec((tc, LANES), lambda i, j: (i * nt + j, 0)),
                  pl.BlockSpec((SUBLANES, tc), lambda i, j: (0, i * nt + j)),
                  pl.BlockSpec((tc, d_attn), lambda i, j: (i * nt + j, 0)),
                  pl.BlockSpec((1, d_attn, tc), lambda i, j: (i, 0, j))],
        out_specs=[pl.BlockSpec((SUBLANES, tc), lambda i, j: (0, i * nt + j)),
                   pl.BlockSpec((tc, n_heads * LANES), lambda i, j: (i * nt + j, 0)),
                   pl.BlockSpec((n_heads * VT_ROWS, tc), lambda i, j: (0, i * nt + j))],
        out_shape=[jax.ShapeDtypeStruct((SUBLANES, n), F32),
                   jax.ShapeDtypeStruct((n, n_heads * LANES), BF16),
                   jax.ShapeDtypeStruct((n_heads * VT_ROWS, n), BF16)],
        scratch_shapes=[pltpu.VMEM((1, LANES), F32), pltpu.VMEM((SUBLANES, 1), F32)],
        compiler_params=_params(("arbitrary", "arbitrary")),
        name="attn_prep",
    )(lf, lft, k, vt_in)


def _attn_kernel(q_ref, ka_ref, vt_ref, cr_ref, o_ref, qt_sc, m_sc, acc_sc, st_sc, *, tq, tk):
    hp = pl.program_id(1)
    qi = pl.program_id(2)
    lane = lax.broadcasted_iota(jnp.int32, (tq, LANES), 1)
    q2 = q_ref[...].astype(F32)
    for e in range(2):
        in_head, aug0 = _head_lanes(lane, e)
        is_aug = (lane >= aug0) & (lane < aug0 + N_AUG)
        qa = jnp.where(in_head, q2, jnp.where(is_aug, -1.0, 0.0))
        qt_sc[e] = qa.T.astype(BF16)
        m_sc[e] = jnp.full((1, tq), NEG_BIG, F32)
        acc_sc[e] = jnp.zeros((VT_ROWS, tq), F32)

    def scores(j, e, q_lo):
        k0 = pl.multiple_of(j * tk, tk)
        ka = ka_ref[pl.ds(k0, tk), LANES * e:LANES * (e + 1)]
        return jnp.dot(ka, qt_sc[e, :, q_lo:], preferred_element_type=F32)

    def consume(st, j, e, diag, q_lo):
        k0 = pl.multiple_of(j * tk, tk)
        if diag is not None:
            kr = lax.broadcasted_iota(jnp.int32, st.shape, 0) + diag * tk
            qc = lax.broadcasted_iota(jnp.int32, st.shape, 1) + q_lo
            st = jnp.where(kr <= qc, st, -jnp.inf)
        cq = cr_ref[pl.ds(2 * hp + e, 1), q_lo:]
        m_prev = m_sc[e, :, q_lo:]
        m_new = jnp.maximum(m_prev, cq + jnp.max(st, axis=0, keepdims=True))
        p = jnp.exp2(st - (m_new - cq))
        alpha = jnp.exp2(m_prev - m_new)
        vt = vt_ref[VT_ROWS * e:VT_ROWS * (e + 1), pl.ds(k0, tk)]
        acc_sc[e, :, q_lo:] = (alpha * acc_sc[e, :, q_lo:]
                               + jnp.dot(vt, p.astype(BF16), preferred_element_type=F32))
        m_sc[e, :, q_lo:] = m_new

    def stage(j, slot, diag, q_lo, next_q_lo):
        for e in range(2):
            if next_q_lo is not None:
                st_sc[1 - slot, e, :, next_q_lo:] = scores(j + 1, e, next_q_lo)
            consume(st_sc[slot, e, :, q_lo:], j, e, diag, q_lo)

    nsub = tq // tk
    for e in range(2):
        st_sc[0, e] = scores(0, e, 0)

    def body(jj, carry):
        for u in range(nsub):
            stage(jj * nsub + u, u % 2, None, 0, 0)
        return carry

    lax.fori_loop(0, qi, body, 0)
    for dd in range(nsub):
        stage(qi * nsub + dd, dd % 2, dd, dd * tk, (dd + 1) * tk if dd + 1 < nsub else None)
    o_t = jnp.concatenate([acc_sc[e][:HEAD_DIM, :] / acc_sc[e][HEAD_DIM:HEAD_DIM + 1, :]
                           for e in range(2)], axis=0)
    o_ref[...] = o_t.T.astype(BF16)


def _attention(q, ka, vt, cr, b, s, tq, tk):
    n, d_attn = q.shape
    assert (tq // tk) % 2 == 0, "the score double-buffer alternates per key block within a q tile"
    nq = s // tq
    npairs = d_attn // LANES
    return pl.pallas_call(
        functools.partial(_attn_kernel, tq=tq, tk=tk),
        grid=(b, npairs, nq),
        in_specs=[pl.BlockSpec((tq, LANES), lambda i, h, j: (i * nq + j, h)),
                  pl.BlockSpec((s, 2 * LANES), lambda i, h, j: (i, h)),
                  pl.BlockSpec((2 * VT_ROWS, s), lambda i, h, j: (h, i)),
                  pl.BlockSpec((SUBLANES, tq), lambda i, h, j: (0, i * nq + j))],
        out_specs=pl.BlockSpec((tq, LANES), lambda i, h, j: (i * nq + j, h)),
        out_shape=jax.ShapeDtypeStruct((n, d_attn), BF16),
        scratch_shapes=[pltpu.VMEM((2, LANES, tq), BF16), pltpu.VMEM((2, 1, tq), F32),
                        pltpu.VMEM((2, VT_ROWS, tq), F32), pltpu.VMEM((2, 2, tk, tq), F32)],
        compiler_params=_params(("arbitrary", "arbitrary", "arbitrary")),
        name="fox_attention",
    )(q, ka, vt, cr)


def _lane_roll_up(x, sh):
    return pltpu.roll(x, x.shape[-1] - sh, x.ndim - 1)


def _sample_attn_kernel(pt_ref, q_ref, kn_ref, vn_ref, lfn_ref, *rest, pages_per_step):
    del pt_ref
    pp = pages_per_step
    kt_refs = rest[:pp]
    vt_refs = rest[pp:2 * pp]
    lp_refs = rest[2 * pp:3 * pp]
    o_ref, m_sc, l_sc, acc_sc, carry_sc = rest[3 * pp:]
    g = pl.program_id(1)
    n_heads, hd, page = kt_refs[0].shape[1:]
    d_attn = n_heads * hd
    own = (lax.broadcasted_iota(jnp.int32, (n_heads, d_attn), 1) // hd
           == lax.broadcasted_iota(jnp.int32, (n_heads, d_attn), 0))
    qbd_f = jnp.where(own, jnp.broadcast_to(q_ref[0], (n_heads, d_attn)), 0.0)
    qbd = qbd_f.astype(BF16)

    @pl.when(g == 0)
    def _():
        kn = kn_ref[0].astype(BF16).astype(F32)
        m_sc[...] = jnp.sum(qbd_f * kn, axis=1, keepdims=True)
        l_sc[...] = jnp.ones_like(l_sc)
        acc_sc[...] = jnp.broadcast_to(vn_ref[0].astype(BF16).astype(F32), (n_heads, d_attn))
        carry_sc[...] = lfn_ref[0]

    lane = lax.broadcasted_iota(jnp.int32, (n_heads, page), 1)
    carry = carry_sc[...]
    scores, values = [], []
    for i in range(pp):
        kt = kt_refs[i][0].reshape(d_attn, page).astype(BF16)
        values.append(vt_refs[i][0].reshape(d_attn, page).astype(BF16))
        lp = lp_refs[i][0]
        suf = lp
        sh = 1
        while sh < page:
            suf = suf + jnp.where(lane + sh < page, _lane_roll_up(suf, sh), 0.0)
            sh *= 2
        scores.append(jnp.dot(qbd, kt, preferred_element_type=F32) + ((suf - lp) + carry))
        carry = carry + suf[:, 0:1]
    carry_sc[...] = carry
    smax = scores[0]
    for s in scores[1:]:
        smax = jnp.maximum(smax, s)
    m_prev = m_sc[...]
    m_new = jnp.maximum(m_prev, jnp.max(smax, axis=1, keepdims=True))
    alpha = jnp.exp(m_prev - m_new)
    psum = None
    pv = None
    for s, vt in zip(scores, values):
        p = jnp.exp(s - m_new)
        psum = p if psum is None else psum + p
        d = lax.dot_general(p.astype(BF16), vt, (((1,), (1,)), ((), ())), preferred_element_type=F32)
        pv = d if pv is None else pv + d
    l_sc[...] = alpha * l_sc[...] + jnp.sum(psum, axis=1, keepdims=True)
    acc_sc[...] = alpha * acc_sc[...] + pv
    m_sc[...] = m_new

    @pl.when(g == pl.num_programs(1) - 1)
    def _():
        o = jnp.where(own, acc_sc[...] / l_sc[...], 0.0)
        o_ref[0] = jnp.sum(o, axis=0, keepdims=True).astype(BF16)


def _sample_attention(q, kn, vn, lfn, ckt, cvt, clt, page_table, pages_per_step):
    db, d_attn = q.shape
    n_heads, hd, page = ckt.shape[1:]
    n_pages = page_table.shape[1]
    pp = pages_per_step
    steps = n_pages // pp
    pt = page_table.reshape(-1)

    def page_map(nd):
        def mk(i):
            return lambda b, g, pt_ref: (pt_ref[b * n_pages + (n_pages - 1 - (g * pp + i))],) + (0,) * nd
        return mk

    tok = lambda shape: pl.BlockSpec((1,) + shape, lambda b, g, pt_ref: (b,) + (0,) * len(shape))
    in_specs = [tok((1, d_attn)), tok((1, d_attn)), tok((1, d_attn)), tok((n_heads, 1))]
    in_specs += [pl.BlockSpec((1, n_heads, hd, page), page_map(3)(i)) for i in range(pp)]
    in_specs += [pl.BlockSpec((1, n_heads, hd, page), page_map(3)(i)) for i in range(pp)]
    in_specs += [pl.BlockSpec((1, n_heads, page), page_map(2)(i)) for i in range(pp)]
    out = pl.pallas_call(
        functools.partial(_sample_attn_kernel, pages_per_step=pp),
        grid_spec=pltpu.PrefetchScalarGridSpec(
            num_scalar_prefetch=1,
            grid=(db, steps),
            in_specs=in_specs,
            out_specs=tok((1, d_attn)),
            scratch_shapes=[pltpu.VMEM((n_heads, 1), F32), pltpu.VMEM((n_heads, 1), F32),
                            pltpu.VMEM((n_heads, d_attn), F32), pltpu.VMEM((n_heads, 1), F32)]),
        out_shape=jax.ShapeDtypeStruct((db, 1, d_attn), BF16),
        compiler_params=_params(("arbitrary", "arbitrary")),
        name="sample_attention",
    )(pt, q.reshape(db, 1, d_attn), kn.reshape(db, 1, d_attn), vn.reshape(db, 1, d_attn), lfn,
      *([ckt] * pp), *([cvt] * pp), *([clt] * pp))
    return out.reshape(db, d_attn)


def _lru_consts(brg, big, a_param):
    return 0.5 * brg, 0.5 * big, (-0.5 * LRU_C * LOG2E) * _softplus(a_param)


def _lru_coeffs(half_pre_r, half_pre_i, xc, half_brg, half_big, k):
    a = jnp.exp2(jnp.tanh(half_pre_r + half_brg) * k + k)
    xh = 0.5 * xc
    ix = jnp.tanh(half_pre_i + half_big) * xh + xh
    mult = jnp.sqrt(1.0 - a * a)
    return a, mult, ix


def _gate_matmul(xcb, wg_ref, pre_ref):
    for j in range(wg_ref.shape[0]):
        lo = j * MXU_DIM
        pre_ref[:, 2 * lo:2 * lo + 2 * MXU_DIM] = jnp.dot(
            xcb[:, lo:lo + MXU_DIM], wg_ref[j], preferred_element_type=F32)


def _rnn_prompt_kernel(x_ref, g_ref, wxr_ref, wgr_ref, wg_ref, brg_ref, big_ref, ap_ref, cw_ref,
                       cb_ref, rnn_ref, conv_ref, h_ref,
                       xr_sc, xc_sc, pre_sc, gg_sc, hcar_sc, vec_sc, *, tt):
    t = pl.program_id(1)
    d = x_ref.shape[1]
    ngroups = tt // SUBLANES

    @pl.when(t == 0)
    def _():
        xr_sc[0:SUBLANES, :] = jnp.zeros((SUBLANES, d), F32)
        hcar_sc[...] = jnp.zeros_like(hcar_sc)

    cw = cw_ref[...]
    vecs = [cw[j:j + 1, :] for j in range(CONV_WIDTH)] + [cb_ref[...]]
    vecs += list(_lru_consts(brg_ref[...], big_ref[...], ap_ref[...]))
    for i, v in enumerate(vecs):
        vec_sc[i] = jnp.broadcast_to(v, (SUBLANES, d))
    i_cb, i_brg, i_big, i_k = range(CONV_WIDTH, CONV_WIDTH + 4)

    xn = _rms(x_ref[...], g_ref[...]).astype(BF16)
    xr_sc[SUBLANES:, :] = jnp.dot(xn, wxr_ref[...], preferred_element_type=F32)
    gg_sc[...] = _gelu_tanh(jnp.dot(xn, wgr_ref[...], preferred_element_type=F32)).astype(BF16)

    sub = lax.broadcasted_iota(jnp.int32, (SUBLANES, d), 0)

    def conv_body(gi, carry):
        r0 = pl.multiple_of(gi * SUBLANES, SUBLANES)
        prev = xr_sc[pl.ds(r0, SUBLANES), :]
        cur = xr_sc[pl.ds(r0 + SUBLANES, SUBLANES), :]
        out = vec_sc[i_cb] + cur * vec_sc[CONV_WIDTH - 1]
        for sh in range(1, CONV_WIDTH):
            shifted = pltpu.roll(jnp.where(sub >= SUBLANES - sh, prev, cur), sh, 0)
            out = out + shifted * vec_sc[CONV_WIDTH - 1 - sh]
        xc_sc[pl.ds(r0, SUBLANES), :] = out
        return carry

    lax.fori_loop(0, ngroups, conv_body, 0)
    conv_ref[0] = xr_sc[tt:tt + SUBLANES, :]
    xr_sc[0:SUBLANES, :] = xr_sc[tt:tt + SUBLANES, :]

    _gate_matmul(xc_sc[...].astype(BF16), wg_ref, pre_sc)

    first_row = (sub == 0) & (t == 0)

    def lru_group(r0, is_first, hc):
        xc = xc_sc[pl.ds(r0, SUBLANES), :]
        pre = pre_sc[pl.ds(r0, SUBLANES), :]
        pre_r = jnp.concatenate([pre[:, 2 * j * MXU_DIM:(2 * j + 1) * MXU_DIM]
                                 for j in range(d // MXU_DIM)], axis=1)
        pre_i = jnp.concatenate([pre[:, (2 * j + 1) * MXU_DIM:(2 * j + 2) * MXU_DIM]
                                 for j in range(d // MXU_DIM)], axis=1)
        a, mult, ix = _lru_coeffs(pre_r, pre_i, xc, vec_sc[i_brg], vec_sc[i_big], vec_sc[i_k])
        if is_first is not None:
            mult = jnp.where(first_row & is_first, 1.0, mult)
        u = mult * ix
        for sh in (1, 2, 4):
            keep = sub >= sh
            a_s = jnp.where(keep, pltpu.roll(a, sh, 0), 1.0)
            u_s = jnp.where(keep, pltpu.roll(u, sh, 0), 0.0)
            u = u + a * u_s
            a = a * a_s
        h = u + a * hc
        return h, jnp.broadcast_to(h[SUBLANES - 1:SUBLANES, :], (SUBLANES, d))

    def lru_body(gi, hc):
        r0 = pl.multiple_of(gi * 2 * SUBLANES, 2 * SUBLANES)
        h0, hc = lru_group(r0, gi == 0, hc)
        h1, hc = lru_group(r0 + SUBLANES, None, hc)
        h = jnp.concatenate([h0, h1], axis=0)
        gg = gg_sc[pl.ds(r0, 2 * SUBLANES), :].astype(F32)
        rnn_ref[pl.ds(r0, 2 * SUBLANES), :] = (h * gg).astype(BF16)
        return hc

    hc = lax.fori_loop(0, ngroups // 2, lru_body, hcar_sc[...])
    hcar_sc[...] = hc
    h_ref[0] = hc


def _rnn_prompt(x, g, wxr, wgr, wg, brg, big, ap, cw, cb, b, s, tt):
    n, d = x.shape
    nt = s // tt
    vec = _const_spec((1, d))
    rnn, conv, h = pl.pallas_call(
        functools.partial(_rnn_prompt_kernel, tt=tt),
        grid=(b, nt),
        in_specs=[pl.BlockSpec((tt, d), lambda i, j: (i * nt + j, 0)), vec,
                  _const_spec(wxr.shape), _const_spec(wgr.shape), _const_spec(wg.shape),
                  vec, vec, vec, _const_spec(cw.shape), vec],
        out_specs=[pl.BlockSpec((tt, d), lambda i, j: (i * nt + j, 0)),
                   pl.BlockSpec((1, SUBLANES, d), lambda i, j: (i, 0, 0)),
                   pl.BlockSpec((1, SUBLANES, d), lambda i, j: (i, 0, 0))],
        out_shape=[jax.ShapeDtypeStruct((n, d), BF16),
                   jax.ShapeDtypeStruct((b, SUBLANES, d), F32),
                   jax.ShapeDtypeStruct((b, SUBLANES, d), F32)],
        scratch_shapes=[pltpu.VMEM((tt + SUBLANES, d), F32), pltpu.VMEM((tt, d), F32),
                        pltpu.VMEM((tt, 2 * d), F32), pltpu.VMEM((tt, d), BF16),
                        pltpu.VMEM((SUBLANES, d), F32),
                        pltpu.VMEM((CONV_WIDTH + 4, SUBLANES, d), F32)],
        compiler_params=_params(("arbitrary", "arbitrary")),
        name="rglru_prompt",
    )(x, g, wxr, wgr, wg, brg, big, ap, cw, cb)
    return rnn, conv[:, SUBLANES - (CONV_WIDTH - 1):, :], h[:, 0, :]


def _rnn_sample_kernel(x_ref, g_ref, wxr_ref, wgr_ref, wg_ref, brg_ref, big_ref, ap_ref, cw_ref,
                       cb_ref, sc_ref, h0_ref, rnn_ref, xr_ref, h_ref, pre_sc):
    d = x_ref.shape[1]
    xn = _rms(x_ref[...], g_ref[...]).astype(BF16)
    xr = jnp.dot(xn, wxr_ref[...], preferred_element_type=F32)
    gg = _gelu_tanh(jnp.dot(xn, wgr_ref[...], preferred_element_type=F32))
    cw = cw_ref[...]
    xc = cb_ref[...] + xr * cw[CONV_WIDTH - 1:CONV_WIDTH, :]
    for j in range(CONV_WIDTH - 1):
        xc = xc + sc_ref[j] * cw[j:j + 1, :]
    _gate_matmul(xc.astype(BF16), wg_ref, pre_sc)
    pre = pre_sc[...]
    pre_r = jnp.concatenate([pre[:, 2 * j * MXU_DIM:(2 * j + 1) * MXU_DIM]
                             for j in range(d // MXU_DIM)], axis=1)
    pre_i = jnp.concatenate([pre[:, (2 * j + 1) * MXU_DIM:(2 * j + 2) * MXU_DIM]
                             for j in range(d // MXU_DIM)], axis=1)
    a, mult, ix = _lru_coeffs(pre_r, pre_i, xc,
                              *_lru_consts(brg_ref[...], big_ref[...], ap_ref[...]))
    h = a * h0_ref[...] + mult * ix
    rnn_ref[...] = (h * gg).astype(BF16)
    xr_ref[...] = xr
    h_ref[...] = h


def _rnn_sample(x, g, wxr, wgr, wg, brg, big, ap, cw, cb, state_conv, h0):
    db, d = x.shape
    full = lambda shape: pl.BlockSpec(shape, lambda i: (0,) * len(shape))
    sc = jnp.swapaxes(state_conv, 0, 1)
    rnn, xr, h = pl.pallas_call(
        _rnn_sample_kernel,
        grid=(1,),
        in_specs=[full(a.shape) for a in (x, g, wxr, wgr, wg, brg, big, ap, cw, cb, sc, h0)],
        out_specs=[full((db, d))] * 3,
        out_shape=[jax.ShapeDtypeStruct((db, d), BF16), jax.ShapeDtypeStruct((db, d), F32),
                   jax.ShapeDtypeStruct((db, d), F32)],
        scratch_shapes=[pltpu.VMEM((db, 2 * d), F32)],
        compiler_params=_params(("arbitrary",)),
        name="rglru_sample",
    )(x, g, wxr, wgr, wg, brg, big, ap, cw, cb, sc, h0)
    conv_new = jnp.concatenate([state_conv[:, 1:, :], xr[:, None, :]], axis=1)
    return rnn, conv_new, h


def _post_kernel(x_ref, attn_ref, rnn_ref, ple_ref, g_pre_ref, wga_ref, wgrr_ref, wao_ref, wro_ref,
                 wout_ref, g_mixpost_ref, g_mlppre_ref, wff1_ref, wff2_ref, g_mlppost_ref,
                 g_plepre_ref, wpg_ref, wple_ref, g_plepost_ref, y_ref, *, ff_chunk):
    x = x_ref[...]
    xn = _rms(x, g_pre_ref[...]).astype(BF16)
    ga = _sigmoid(jnp.dot(xn, wga_ref[...], preferred_element_type=F32))
    m = ga * jnp.dot(attn_ref[...], wao_ref[...], preferred_element_type=F32)
    gr = _sigmoid(jnp.dot(xn, wgrr_ref[...], preferred_element_type=F32))
    m = m + gr * jnp.dot(rnn_ref[...], wro_ref[...], preferred_element_type=F32)
    x = x + _rms(jnp.dot(m.astype(BF16), wout_ref[...], preferred_element_type=F32),
                 g_mixpost_ref[...])
    hn = _rms(x, g_mlppre_ref[...]).astype(BF16)
    d_ff = wff1_ref.shape[1]
    f = None
    for c in range(d_ff // ff_chunk):
        lo = c * ff_chunk
        hid = jnp.dot(hn, wff1_ref[:, lo:lo + ff_chunk], preferred_element_type=F32)
        hid = jnp.square(jnp.maximum(hid, 0.0)).astype(BF16)
        part = jnp.dot(hid, wff2_ref[lo:lo + ff_chunk, :], preferred_element_type=F32)
        f = part if f is None else f + part
    x = x + _rms(f, g_mlppost_ref[...])
    gate = _sigmoid(jnp.dot(_rms(x, g_plepre_ref[...]).astype(BF16), wpg_ref[...],
                            preferred_element_type=F32))
    pe = jnp.dot(ple_ref[...].astype(BF16), wple_ref[...], preferred_element_type=F32)
    y_ref[...] = x + _rms(pe * gate, g_plepost_ref[...])


def _post(x, attn, rnn, ple, vecs, mats, tm):
    n, d = x.shape
    tok = lambda w: pl.BlockSpec((tm, w), lambda i: (i, 0))
    g_pre, g_mixpost, g_mlppre, g_mlppost, g_plepre, g_plepost = vecs
    wga, wgrr, wao, wro, wout, wff1, wff2, wpg, wple = mats
    vec = _const_spec((1, d))
    ops = (x, attn, rnn, ple, g_pre, wga, wgrr, wao, wro, wout, g_mixpost, g_mlppre, wff1, wff2,
           g_mlppost, g_plepre, wpg, wple, g_plepost)
    in_specs = [tok(d), tok(attn.shape[1]), tok(d), tok(ple.shape[1])]
    in_specs += [vec if a.shape == (1, d) else _const_spec(a.shape) for a in ops[4:]]
    return pl.pallas_call(
        functools.partial(_post_kernel, ff_chunk=min(1024, wff1.shape[1])),
        grid=(n // tm,),
        in_specs=in_specs,
        out_specs=tok(d),
        out_shape=jax.ShapeDtypeStruct((n, d), F32),
        compiler_params=_params(("arbitrary",)),
        name="merge_mlp_ple",
    )(*ops)


def _gate_blockdiag(w_rg, w_ig):
    nb = w_rg.shape[0]
    per = MXU_DIM // RNN_BLOCK
    eye = jnp.eye(per, dtype=F32)

    def bd(w):
        w = w.reshape(nb // per, per, RNN_BLOCK, RNN_BLOCK)
        t = jnp.einsum('jade,ab->jadbe', w, eye)
        return t.reshape(nb // per, MXU_DIM, MXU_DIM)

    return (0.5 * jnp.concatenate([bd(w_rg), bd(w_ig)], axis=2)).astype(BF16)


def _pick(n, pref):
    t = min(n, pref)
    while n % t:
        t //= 2
    return t


def kernel(x_prompt, x_sample, p_prompt, p_sample, cache_k, cache_v, cache_logf, state_conv, state_h,
           page_table, g_mix_pre, w_in, b_f, w_rg, b_rg, w_ig, b_ig, a_param, conv_w, conv_b,
           w_attn_out, w_rnn_out, w_out, g_mix_post, g_mlp_pre, w_ff1, w_ff2, g_mlp_post,
           g_ple_pre, w_ple_gate, w_ple, g_ple_post):
    depth = w_in.shape[0]
    b, s, d = x_prompt.shape
    db = x_sample.shape[0]
    n_heads = b_f.shape[1]
    d_attn = n_heads * HEAD_DIM
    d_rnn = w_rg.shape[1] * RNN_BLOCK
    assert x_sample.shape[1] == 1 and d_rnn == d and n_heads == SUBLANES

    xp = x_prompt.reshape(b * s, d)
    xs = x_sample.reshape(db, d)
    outs = [[] for _ in range(10)]
    for l in range(depth):
        row = lambda a: a[l].reshape(1, -1)
        wl = w_in[l]
        o = 3 * d_attn
        wqkv = wl[:, :o].astype(BF16)
        wf = jnp.pad(wl[:, o:o + n_heads], ((0, 0), (0, LANES - n_heads))).astype(BF16)
        bfp = jnp.pad(b_f[l].reshape(1, -1), ((0, 0), (0, LANES - n_heads)))
        o += n_heads
        wxr = wl[:, o:o + d_rnn].astype(BF16)
        wgr = wl[:, o + d_rnn:o + 2 * d_rnn].astype(BF16)
        o += 2 * d_rnn
        wga = wl[:, o:o + d].astype(BF16)
        wgrr = wl[:, o + d:o + 2 * d].astype(BF16)
        wg = _gate_blockdiag(w_rg[l], w_ig[l])
        rnn_w = (row(g_mix_pre), wxr, wgr, wg, row(b_rg), row(b_ig), row(a_param), conv_w[l],
                 row(conv_b))
        vecs = (row(g_mix_pre), row(g_mix_post), row(g_mlp_pre), row(g_mlp_post), row(g_ple_pre),
                row(g_ple_post))
        mats = (wga, wgrr, w_attn_out[l].astype(BF16), w_rnn_out[l].astype(BF16),
                w_out[l].astype(BF16), w_ff1[l].astype(BF16), w_ff2[l].astype(BF16),
                w_ple_gate[l].astype(BF16), w_ple[l].astype(BF16))

        tm = _pick(s, 512)
        inv_sqrt_d = 1.0 / math.sqrt(HEAD_DIM)
        q, k, k_cm, v_cm, lf, lft = _inproj(xp, row(g_mix_pre), wqkv, wf, bfp, tm,
                                            LOG2E * inv_sqrt_d, (b, s))
        cr, ka, vt = _prep(lf, lft, k, v_cm, b, s, _pick(s, 512))
        tk = _pick(s, 256)
        attn = _attention(q, ka, vt, cr, b, s, _pick(s, 4 * tk), tk)
        rnn, conv_p, h_p = _rnn_prompt(xp, *rnn_w, b, s, _pick(s, 512))
        xp = _post(xp, attn, rnn, p_prompt[l].reshape(b * s, -1), vecs, mats, tm)
        seq_major = lambda a: jnp.transpose(a.reshape(b, n_heads, HEAD_DIM, s), (0, 3, 1, 2))
        for i, a in enumerate((seq_major(k_cm), seq_major(v_cm),
                               lf[:, :n_heads].reshape(b, s, n_heads), conv_p, h_p)):
            outs[i].append(a)

        qs, ks, vs, lfs, _ = _inproj(xs, row(g_mix_pre), wqkv, wf, bfp, db, inv_sqrt_d)
        lfs = lfs[:, :n_heads]
        attn_s = _sample_attention(
            qs.astype(F32), ks, vs, lfs.reshape(db, n_heads, 1),
            jnp.transpose(cache_k[l], (0, 2, 3, 1)), jnp.transpose(cache_v[l], (0, 2, 3, 1)),
            jnp.transpose(cache_logf[l], (0, 2, 1)), page_table, _pick(page_table.shape[1], 32))
        rnn_s, conv_s, h_s = _rnn_sample(xs, *rnn_w, state_conv[l], state_h[l].astype(F32))
        xs = _post(xs, attn_s, rnn_s, p_sample[l].reshape(db, -1), vecs, mats, db)
        for i, a in enumerate((ks.reshape(db, 1, n_heads, HEAD_DIM), vs.reshape(db, 1, n_heads, HEAD_DIM),
                               lfs.reshape(db, 1, n_heads), conv_s, h_s)):
            outs[5 + i].append(a)

    return (xp.reshape(b, s, d), xs.reshape(db, 1, d), *[jnp.stack(o) for o in outs])
```
